```python
import jax, jax.numpy as jnp
from jax import lax
import numpy as np

D_MODEL = 2048
BATCH = 4
SEQ = 2048
DEPTH = 4
DEC_BATCH = 8
DEC_SEQ = 4
PAST_LEN = 16384
PAGE_SIZE = 128

HEAD_DIM = 128
N_HEADS_TOTAL = D_MODEL // HEAD_DIM
N_MEM_HEADS = 4
N_SELF_HEADS = N_HEADS_TOTAL - N_MEM_HEADS
N_MEM = 256
DIL_GROUPS = ((128, 1), (512, 4), (2048, 16))
N_GROUPS = len(DIL_GROUPS)
HEADS_PER_GROUP = N_SELF_HEADS // N_GROUPS
WIN_ROWS = sum(w for w, _ in DIL_GROUPS)
Q_BLOCK = 128
N_MIXERS = 2
N_A_LAYERS = (DEPTH + N_MIXERS - 1) // N_MIXERS
N_B_LAYERS = DEPTH // N_MIXERS
SELF_W = N_SELF_HEADS * HEAD_DIM
MEM_W = N_MEM_HEADS * HEAD_DIM
IN_W_A = 3 * SELF_W + N_SELF_HEADS + MEM_W
IN_W_B = 3 * SELF_W + MEM_W
N_EXPERTS = 32
TOP_K = 4
D_FF = D_MODEL
SWIGLU_ALPHA = 1.702
SWIGLU_LIMIT = 7.0
MOE_BLOCK = 128
DEEPNORM_ALPHA = (2 * DEPTH) ** 0.25
DEEPNORM_BETA = (8 * DEPTH) ** -0.25
LN_EPS = 1e-5
FGATE_BIAS = 3.0
ATTN_SCALE = HEAD_DIM ** -0.5

kernel_name = 'fox_dilated_memory_moe_deepnorm_step'


def layer_norm(x, g, b):
    xf = x.astype(jnp.float32)
    xc = xf - jnp.mean(xf, axis=-1, keepdims=True)
    var = jnp.mean(xc * xc, axis=-1, keepdims=True)
    y = xc * lax.rsqrt(var + LN_EPS) * g.astype(jnp.float32) + b.astype(jnp.float32)
    return y.astype(x.dtype)


def alibi_slopes(n):
    return 2.0 ** (-8.0 * jnp.arange(1, n + 1, dtype=jnp.float32) / n)


def split_heads(t, h):
    return t.reshape(t.shape[:-1] + (h, HEAD_DIM))


def project_fox(x, w_in, b_f):
    p = x @ w_in
    q = split_heads(p[..., :SELF_W], N_SELF_HEADS)
    k = split_heads(p[..., SELF_W:2 * SELF_W], N_SELF_HEADS)
    v = split_heads(p[..., 2 * SELF_W:3 * SELF_W], N_SELF_HEADS)
    f_logit = p[..., 3 * SELF_W:3 * SELF_W + N_SELF_HEADS]
    qm = split_heads(p[..., 3 * SELF_W + N_SELF_HEADS:], N_MEM_HEADS)
    logf = jax.nn.log_sigmoid(f_logit.astype(jnp.float32) + b_f.astype(jnp.float32))
    return q, k, v, logf, qm


def project_dilated(x, w_in):
    p = x @ w_in
    q = split_heads(p[..., :SELF_W], N_SELF_HEADS)
    k = split_heads(p[..., SELF_W:2 * SELF_W], N_SELF_HEADS)
    v = split_heads(p[..., 2 * SELF_W:3 * SELF_W], N_SELF_HEADS)
    qm = split_heads(p[..., 3 * SELF_W:], N_MEM_HEADS)
    return q, k, v, qm


def fox_attend(qb, cq_t, q_pos, k, v, ck_t, k_pos):
    s = jnp.einsum('bqhd,bkhd->bhqk', qb, k, preferred_element_type=jnp.float32) * ATTN_SCALE
    s = s + (cq_t[..., :, None] - ck_t[..., None, :])
    s = jnp.where(k_pos[None, :] <= q_pos[:, None], s, -jnp.inf)
    p = jax.nn.softmax(s, axis=-1)
    return jnp.einsum('bhqk,bkhd->bqhd', p.astype(v.dtype), v)


def fox_prompt(q, k, v, logf):
    B, S = q.shape[:2]
    c_t = jnp.cumsum(logf, axis=1).transpose(0, 2, 1)
    k_pos = jnp.arange(S)

    def block(i):
        q0 = i * Q_BLOCK
        qb = lax.dynamic_slice_in_dim(q, q0, Q_BLOCK, axis=1)
        cq = lax.dynamic_slice_in_dim(c_t, q0, Q_BLOCK, axis=2)
        return fox_attend(qb, cq, q0 + jnp.arange(Q_BLOCK), k, v, c_t, k_pos)

    o = lax.map(block, jnp.arange(S // Q_BLOCK))
    return o.transpose(1, 0, 2, 3, 4).reshape(B, S, N_SELF_HEADS, HEAD_DIM)


def fox_sample(q, k, v, logf, k_past, v_past, logf_past):
    P, n = k_past.shape[1], q.shape[1]
    k_all = jnp.concatenate([k_past, k], axis=1)
    v_all = jnp.concatenate([v_past, v], axis=1)
    lf_all = jnp.concatenate([logf_past.astype(jnp.float32), logf], axis=1)
    c_t = jnp.cumsum(lf_all, axis=1).transpose(0, 2, 1)
    return fox_attend(q, c_t[..., P:], P + jnp.arange(n), k_all, v_all, c_t, jnp.arange(P + n))


def dilated_attend(q, k, v, q_loc0, q_pos0, window, dilation, slopes):
    n = q.shape[1]
    dist = jnp.arange(window // dilation + 1) * dilation
    q_loc = q_loc0 + jnp.arange(n)
    q_pos = q_pos0 + jnp.arange(n)
    idx = jnp.clip(q_loc[:, None] - dist[None, :], 0, k.shape[1] - 1)
    valid = (q_pos[:, None] - dist[None, :]) >= 0
    kg = jnp.take(k, idx, axis=1)
    vg = jnp.take(v, idx, axis=1)
    s = jnp.einsum('bqhd,bqkhd->bhqk', q, kg, preferred_element_type=jnp.float32) * ATTN_SCALE
    s = s - slopes[:, None, None] * dist.astype(jnp.float32)[None, None, :]
    s = jnp.where(valid[None, None], s, -jnp.inf)
    lse = jax.nn.logsumexp(s, axis=-1)
    p = jnp.exp(s - lse[..., None])
    o = jnp.einsum('bhqk,bqkhd->bqhd', p.astype(v.dtype), vg)
    return o, lse.transpose(0, 2, 1)


def combine_groups(outs, lses):
    w = jax.nn.softmax(jnp.stack(lses), axis=0)
    o = jnp.stack(outs) * w[..., None].astype(outs[0].dtype)
    B, n = o.shape[1], o.shape[2]
    return o.transpose(1, 2, 0, 3, 4).reshape(B, n, N_SELF_HEADS, HEAD_DIM)


def dilated_prompt(q, k, v, slopes):
    B, S = q.shape[:2]

    def block(i):
        q0 = i * Q_BLOCK
        qb = lax.dynamic_slice_in_dim(q, q0, Q_BLOCK, axis=1)
        outs, lses = [], []
        for g, (w, d) in enumerate(DIL_GROUPS):
            hs = slice(g * HEADS_PER_GROUP, (g + 1) * HEADS_PER_GROUP)
            o, l = dilated_attend(qb[:, :, hs], k[:, :, hs], v[:, :, hs], q0, q0, w, d, slopes[hs])
            outs.append(o)
            lses.append(l)
        return combine_groups(outs, lses)

    o = lax.map(block, jnp.arange(S // Q_BLOCK))
    o = o.transpose(1, 0, 2, 3, 4).reshape(B, S, N_SELF_HEADS, HEAD_DIM)
    bk, bv = [], []
    for g, (w, d) in enumerate(DIL_GROUPS):
        hs = slice(g * HEADS_PER_GROUP, (g + 1) * HEADS_PER_GROUP)
        pad = ((0, 0), (w, 0), (0, 0), (0, 0))
        bk.append(jnp.pad(k[:, :, hs], pad)[:, -w:])
        bv.append(jnp.pad(v[:, :, hs], pad)[:, -w:])
    return o, jnp.concatenate(bk, axis=1), jnp.concatenate(bv, axis=1)


def dilated_sample(q, k, v, buf_k, buf_v, past_len, slopes):
    n = q.shape[1]
    outs, lses, bk, bv = [], [], [], []
    off = 0
    for g, (w, d) in enumerate(DIL_GROUPS):
        hs = slice(g * HEADS_PER_GROUP, (g + 1) * HEADS_PER_GROUP)
        k_ext = jnp.concatenate([buf_k[:, off:off + w], k[:, :, hs]], axis=1)
        v_ext = jnp.concatenate([buf_v[:, off:off + w], v[:, :, hs]], axis=1)
        o, l = dilated_attend(q[:, :, hs], k_ext, v_ext, w, past_len, w, d, slopes[hs])
        outs.append(o)
        lses.append(l)
        bk.append(k_ext[:, n:])
        bv.append(v_ext[:, n:])
        off += w
    return combine_groups(outs, lses), jnp.concatenate(bk, axis=1), jnp.concatenate(bv, axis=1)


def mem_attend(qm, mk, mv):
    s = jnp.einsum('bqhd,bmhd->bhqm', qm, mk, preferred_element_type=jnp.float32) * ATTN_SCALE
    p = jax.nn.softmax(s, axis=-1)
    return jnp.einsum('bhqm,bmhd->bqhd', p.astype(mv.dtype), mv)


def moe_ffn(x, router_w, router_b, w1, b1, w2, b2):
    T, D = x.shape
    logits = (x @ router_w).astype(jnp.float32) + router_b.astype(jnp.float32)
    top_val, top_idx = lax.top_k(logits, TOP_K)
    gates = jax.nn.softmax(top_val, axis=-1).astype(x.dtype)
    A = T * TOP_K
    e_flat = top_idx.reshape(A).astype(jnp.int32)
    tok_flat = jnp.arange(A, dtype=jnp.int32) // TOP_K
    counts = jnp.bincount(e_flat, length=N_EXPERTS).astype(jnp.int32)
    padded = (counts + MOE_BLOCK - 1) // MOE_BLOCK * MOE_BLOCK
    start = jnp.cumsum(counts) - counts
    pend = jnp.cumsum(padded)
    pstart = pend - padded
    order = jnp.argsort(e_flat)
    e_sorted = e_flat[order]
    dest_sorted = (pstart[e_sorted] + jnp.arange(A, dtype=jnp.int32) - start[e_sorted]).astype(jnp.int32)
    dest = jnp.zeros((A,), jnp.int32).at[order].set(dest_sorted)
    n_blocks = -(-(A + N_EXPERTS * (MOE_BLOCK - 1)) // MOE_BLOCK)
    n_rows = n_blocks * MOE_BLOCK
    row_tok = jnp.full((n_rows,), T, jnp.int32).at[dest].set(tok_flat)
    blk_e = jnp.minimum(jnp.searchsorted(pend, jnp.arange(n_blocks, dtype=jnp.int32) * MOE_BLOCK,
                                         side='right'), N_EXPERTS - 1).astype(jnp.int32)
    x_pad = jnp.concatenate([x, jnp.zeros((1, D), x.dtype)], axis=0)
    xb = x_pad[row_tok].reshape(n_blocks, MOE_BLOCK, D)

    def expert_block(args):
        xs, e = args
        h = xs @ w1[e] + b1[e]
        g = jnp.minimum(h[:, :D_FF], SWIGLU_LIMIT)
        lin = jnp.clip(h[:, D_FF:], -SWIGLU_LIMIT, SWIGLU_LIMIT)
        a = g * jax.nn.sigmoid(SWIGLU_ALPHA * g) * (lin + 1.0)
        return a @ w2[e] + b2[e]

    out = lax.map(expert_block, (xb, blk_e)).reshape(n_rows, D)
    return jnp.einsum('tkd,tk->td', out[dest].reshape(T, TOP_K, D), gates)


def setup_inputs(seed: int = 0) -> dict:
    key = jax.random.key(seed)
    ks = jax.random.split(key, 32)
    n_pages = PAST_LEN // PAGE_SIZE
    used = DEC_BATCH * n_pages
    n_pool = used + max(1, used // 4)
    sd = D_MODEL ** -0.5

    def nrm(k, shape, s=1.0):
        return s * jax.random.normal(k, shape, jnp.float32)

    page_table = jax.random.permutation(ks[9], n_pool)[:used].reshape(DEC_BATCH, n_pages).astype(jnp.int32)
    return {
        'x_prompt': nrm(ks[0], (BATCH, SEQ, D_MODEL)),
        'x_sample': nrm(ks[1], (DEC_BATCH, DEC_SEQ, D_MODEL)),
        'cache_fox_k': nrm(ks[2], (N_A_LAYERS, n_pool, PAGE_SIZE, N_SELF_HEADS, HEAD_DIM)),
        'cache_fox_v': nrm(ks[3], (N_A_LAYERS, n_pool, PAGE_SIZE, N_SELF_HEADS, HEAD_DIM)),
        'cache_fox_logf': jax.nn.log_sigmoid(FGATE_BIAS + nrm(ks[4], (N_A_LAYERS, n_pool, PAGE_SIZE, N_SELF_HEADS))),
        'cache_win_k': nrm(ks[5], (N_B_LAYERS, DEC_BATCH, WIN_ROWS, HEADS_PER_GROUP, HEAD_DIM)),
        'cache_win_v': nrm(ks[6], (N_B_LAYERS, DEC_BATCH, WIN_ROWS, HEADS_PER_GROUP, HEAD_DIM)),
        'cache_mem_k': nrm(ks[7], (DEPTH, DEC_BATCH, N_MEM, N_MEM_HEADS, HEAD_DIM)),
        'cache_mem_v': nrm(ks[8], (DEPTH, DEC_BATCH, N_MEM, N_MEM_HEADS, HEAD_DIM)),
        'page_table': page_table,
        'mem_prompt': nrm(ks[10], (BATCH, N_MEM, D_MODEL)),
        'w_in_a': nrm(ks[11], (N_A_LAYERS, D_MODEL, IN_W_A), sd),
        'b_fgate': FGATE_BIAS + nrm(ks[12], (N_A_LAYERS, N_SELF_HEADS), 0.1),
        'w_in_b': nrm(ks[13], (N_B_LAYERS, D_MODEL, IN_W_B), sd),
        'w_mem_kv': nrm(ks[14], (DEPTH, D_MODEL, 2 * MEM_W), sd),
        'w_out': nrm(ks[15], (DEPTH, N_HEADS_TOTAL * HEAD_DIM, D_MODEL), sd * DEEPNORM_BETA),
        'ln1_g': 1.0 + nrm(ks[16], (DEPTH, D_MODEL), 0.05),
        'ln1_b': nrm(ks[17], (DEPTH, D_MODEL), 0.02),
        'router_w': nrm(ks[18], (DEPTH, D_MODEL, N_EXPERTS), sd),
        'router_b': nrm(ks[19], (DEPTH, N_EXPERTS), 0.01),
        'w1': nrm(ks[20], (DEPTH, N_EXPERTS, D_MODEL, 2 * D_FF), sd),
        'b1': nrm(ks[21], (DEPTH, N_EXPERTS, 2 * D_FF), 0.02),
        'w2': nrm(ks[22], (DEPTH, N_EXPERTS, D_FF, D_MODEL), D_FF ** -0.5 * DEEPNORM_BETA),
        'b2': nrm(ks[23], (DEPTH, N_EXPERTS, D_MODEL), 0.02),
        'ln2_g': 1.0 + nrm(ks[24], (DEPTH, D_MODEL), 0.05),
        'ln2_b': nrm(ks[25], (DEPTH, D_MODEL), 0.02),
    }


def reference(x_prompt, x_sample, cache_fox_k, cache_fox_v, cache_fox_logf, cache_win_k, cache_win_v,
              cache_mem_k, cache_mem_v, page_table, mem_prompt, w_in_a, b_fgate, w_in_b, w_mem_kv, w_out,
              ln1_g, ln1_b, router_w, router_b, w1, b1, w2, b2, ln2_g, ln2_b):
    B, S, D = x_prompt.shape
    DB, n, _ = x_sample.shape
    P = page_table.shape[1] * cache_fox_k.shape[2]
    slopes = alibi_slopes(N_SELF_HEADS)
    yp, ys = x_prompt, x_sample
    fk_p, fv_p, flf_p, fk_s, fv_s, flf_s = [], [], [], [], [], []
    wk_p, wv_p, wk_s, wv_s = [], [], [], []
    mk_list, mv_list = [], []
    for i in range(DEPTH):
        j = i // N_MIXERS
        mkv = mem_prompt @ w_mem_kv[i]
        mk_p = split_heads(mkv[..., :MEM_W], N_MEM_HEADS)
        mv_p = split_heads(mkv[..., MEM_W:], N_MEM_HEADS)
        mk_list.append(mk_p)
        mv_list.append(mv_p)
        if i % N_MIXERS == 0:
            q_p, k_p, v_p, lf_p, qm_p = project_fox(yp, w_in_a[j], b_fgate[j])
            q_s, k_s, v_s, lf_s, qm_s = project_fox(ys, w_in_a[j], b_fgate[j])
            o_p = fox_prompt(q_p, k_p, v_p, lf_p)
            k_past = cache_fox_k[j][page_table].reshape(DB, P, N_SELF_HEADS, HEAD_DIM)
            v_past = cache_fox_v[j][page_table].reshape(DB, P, N_SELF_HEADS, HEAD_DIM)
            lf_past = cache_fox_logf[j][page_table].reshape(DB, P, N_SELF_HEADS)
            o_s = fox_sample(q_s, k_s, v_s, lf_s, k_past, v_past, lf_past)
            fk_p.append(k_p)
            fv_p.append(v_p)
            flf_p.append(lf_p)
            fk_s.append(k_s)
            fv_s.append(v_s)
            flf_s.append(lf_s)
        else:
            q_p, k_p, v_p, qm_p = project_dilated(yp, w_in_b[j])
            q_s, k_s, v_s, qm_s = project_dilated(ys, w_in_b[j])
            o_p, bk_p, bv_p = dilated_prompt(q_p, k_p, v_p, slopes)
            o_s, bk_s, bv_s = dilated_sample(q_s, k_s, v_s, cache_win_k[j], cache_win_v[j], P, slopes)
            wk_p.append(bk_p)
            wv_p.append(bv_p)
            wk_s.append(bk_s)
            wv_s.append(bv_s)
        om_p = mem_attend(qm_p, mk_p, mv_p)
        om_s = mem_attend(qm_s, cache_mem_k[i], cache_mem_v[i])
        h_p = jnp.concatenate([o_p, om_p], axis=2).reshape(B, S, D) @ w_out[i]
        h_s = jnp.concatenate([o_s, om_s], axis=2).reshape(DB, n, D) @ w_out[i]
        yp = layer_norm(DEEPNORM_ALPHA * yp + h_p, ln1_g[i], ln1_b[i])
        ys = layer_norm(DEEPNORM_ALPHA * ys + h_s, ln1_g[i], ln1_b[i])
        tok = jnp.concatenate([yp.reshape(B * S, D), ys.reshape(DB * n, D)], axis=0)
        f = moe_ffn(tok, router_w[i], router_b[i], w1[i], b1[i], w2[i], b2[i])
        tok = layer_norm(DEEPNORM_ALPHA * tok + f, ln2_g[i], ln2_b[i])
        yp = tok[:B * S].reshape(B, S, D)
        ys = tok[B * S:].reshape(DB, n, D)
    return (yp, ys, jnp.stack(fk_p), jnp.stack(fv_p), jnp.stack(flf_p), jnp.stack(wk_p), jnp.stack(wv_p),
            jnp.stack(mk_list), jnp.stack(mv_list), jnp.stack(fk_s), jnp.stack(fv_s), jnp.stack(flf_s),
            jnp.stack(wk_s), jnp.stack(wv_s))
```

```python
import functools

import numpy as np
import jax
import jax.numpy as jnp
from jax import lax
from jax.experimental import pallas as pl
from jax.experimental.pallas import tpu as pltpu

HEAD_DIM = 128
N_SELF_HEADS = 12
N_MEM_HEADS = 4
DIL_GROUPS = ((128, 1), (512, 4), (2048, 16))
HEADS_PER_GROUP = N_SELF_HEADS // len(DIL_GROUPS)
N_EXPERTS = 32
TOP_K = 4
SWIGLU_ALPHA = 1.702
SWIGLU_LIMIT = 7.0
N_MIXERS = 2
LN_EPS = 1e-5
ATTN_SCALE = HEAD_DIM ** -0.5
SELF_W = N_SELF_HEADS * HEAD_DIM
MEM_W = N_MEM_HEADS * HEAD_DIM
GROUP_W = HEADS_PER_GROUP * HEAD_DIM

LANES = 128
SUBLANES = 8
ROW_TILE = 512
COL_TILE = 512
ATTN_TILE = 512
MOE_TILE = 256
PAGES_PER_STEP = 4
VMEM_LIMIT = 56 * 1024 * 1024

BF16 = jnp.bfloat16
F32 = jnp.float32
NEG_INF = float("-inf")


def _params(*sem):
    return pltpu.CompilerParams(dimension_semantics=sem, vmem_limit_bytes=VMEM_LIMIT)


def _dot(a, b):
    return jnp.dot(a, b, preferred_element_type=F32)


def _dot_nt(a, b):
    return lax.dot_general(a, b, (((1,), (1,)), ((), ())), preferred_element_type=F32)


def _scan_lanes(v):
    lane = lax.broadcasted_iota(jnp.int32, v.shape, 1)
    shift = 1
    while shift < LANES:
        v = v + jnp.where(lane >= shift, pltpu.roll(v, shift, axis=1), 0.0)
        shift *= 2
    return v


def _layer_norm(r, g, b):
    mu = jnp.mean(r, axis=-1, keepdims=True)
    xc = r - mu
    var = jnp.mean(xc * xc, axis=-1, keepdims=True)
    return xc * lax.rsqrt(var + LN_EPS) * g + b


def _proj_kernel(x_ref, w_ref, o_ref, wbf_ref):
    @pl.when(pl.program_id(1) == 0)
    def _():
        wbf_ref[...] = w_ref[...].astype(BF16)

    o_ref[...] = _dot(x_ref[...], wbf_ref[...])


def _proj(x16, w, layer, n_cols):
    t, k = x16.shape
    return pl.pallas_call(
        _proj_kernel,
        out_shape=jax.ShapeDtypeStruct((t, n_cols), F32),
        grid=(n_cols // COL_TILE, t // ROW_TILE),
        in_specs=[pl.BlockSpec((ROW_TILE, k), lambda j, i: (i, 0)),
                  pl.BlockSpec((None, k, COL_TILE), lambda j, i: (layer, 0, j))],
        out_specs=pl.BlockSpec((ROW_TILE, COL_TILE), lambda j, i: (i, j)),
        scratch_shapes=[pltpu.VMEM((k, COL_TILE), BF16)],
        compiler_params=_params("arbitrary", "arbitrary"),
        name="proj")(x16, w)


def _memkv_kernel(x_ref, w_ref, o_ref):
    o_ref[...] = _dot(x_ref[...].astype(BF16), w_ref[...].astype(BF16))


def _mem_kv(mem2d, w_mem_kv):
    r, d = mem2d.shape
    depth = w_mem_kv.shape[0]
    tm = min(ROW_TILE, r)
    assert r % tm == 0
    out = pl.pallas_call(
        _memkv_kernel,
        out_shape=jax.ShapeDtypeStruct((depth, 2, r, MEM_W), F32),
        grid=(depth, 2, r // tm),
        in_specs=[pl.BlockSpec((tm, d), lambda l, c, i: (i, 0)),
                  pl.BlockSpec((None, d, MEM_W), lambda l, c, i: (l, 0, c))],
        out_specs=pl.BlockSpec((None, None, tm, MEM_W), lambda l, c, i: (l, c, i, 0)),
        compiler_params=_params("arbitrary", "arbitrary", "arbitrary"),
        name="mem_kv")(mem2d, w_mem_kv)
    return out[:, 0], out[:, 1]


def _gate_kernel(x_ref, wf_ref, bf_ref, lf_ref, c_ref):
    z = _dot_nt(wf_ref[...], x_ref[...]) + bf_ref[...]
    lf = jnp.minimum(z, 0.0) - jnp.log1p(jnp.exp(-jnp.abs(z)))
    lf_ref[...] = lf
    carry = jnp.zeros((lf.shape[0], 1), F32)
    for blk in range(lf.shape[1] // LANES):
        c = _scan_lanes(lf[:, blk * LANES:(blk + 1) * LANES]) + carry
        c_ref[:, blk * LANES:(blk + 1) * LANES] = c
        carry = c[:, LANES - 1:LANES]


def _gate(x16, wf_t, b_col, blk_rows, first_block, n_blocks):
    k = x16.shape[1]
    hp = wf_t.shape[0]
    shp = jax.ShapeDtypeStruct((n_blocks, hp, blk_rows), F32)
    return pl.pallas_call(
        _gate_kernel,
        out_shape=(shp, shp),
        grid=(n_blocks,),
        in_specs=[pl.BlockSpec((blk_rows, k), lambda b: (first_block + b, 0)),
                  pl.BlockSpec((hp, k), lambda b: (0, 0)),
                  pl.BlockSpec((hp, 1), lambda b: (0, 0))],
        out_specs=(pl.BlockSpec((None, hp, blk_rows), lambda b: (b, 0, 0)),
                   pl.BlockSpec((None, hp, blk_rows), lambda b: (b, 0, 0))),
        compiler_params=_params("arbitrary"),
        name="forget_gate")(x16, wf_t, b_col)


def _fox_prompt_kernel(q_ref, k_ref, v_ref, ck_ref, o_ref, m_ref, l_ref, acc_ref):
    qi = pl.program_id(2)
    ki = pl.program_id(3)
    tq = q_ref.shape[0]

    @pl.when(ki == 0)
    def _():
        m_ref[...] = jnp.full(m_ref.shape, NEG_INF, F32)
        l_ref[...] = jnp.zeros(l_ref.shape, F32)
        acc_ref[...] = jnp.zeros(acc_ref.shape, F32)

    @pl.when(ki <= qi)
    def _():
        s = _dot_nt(q_ref[...].astype(BF16), k_ref[...].astype(BF16)) * ATTN_SCALE
        s = s - ck_ref[...]
        row = lax.broadcasted_iota(jnp.int32, s.shape, 0) + (qi - ki) * tq
        col = lax.broadcasted_iota(jnp.int32, s.shape, 1)
        s = jnp.where(col <= row, s, NEG_INF)
        m_prev = m_ref[...]
        m_new = jnp.maximum(m_prev, jnp.max(s, axis=1, keepdims=True))
        alpha = jnp.exp(m_prev - m_new)
        p = jnp.exp(s - m_new)
        l_ref[...] = alpha * l_ref[...] + jnp.sum(p, axis=1, keepdims=True)
        acc_ref[...] = alpha * acc_ref[...] + _dot(p.astype(BF16), v_ref[...].astype(BF16))
        m_ref[...] = m_new

    @pl.when(ki == qi)
    def _():
        o_ref[...] = (acc_ref[...] / l_ref[...]).astype(o_ref.dtype)


def _fox_prompt(qkv, ck, n_rows, batch, seq):
    nq = seq // ATTN_TILE
    return pl.pallas_call(
        _fox_prompt_kernel,
        out_shape=jax.ShapeDtypeStruct((n_rows, SELF_W), BF16),
        grid=(batch, N_SELF_HEADS, nq, nq),
        in_specs=[
            pl.BlockSpec((ATTN_TILE, HEAD_DIM), lambda b, h, qi, ki: (b * nq + qi, h)),
            pl.BlockSpec((ATTN_TILE, HEAD_DIM),
                         lambda b, h, qi, ki: (b * nq + jnp.minimum(ki, qi), N_SELF_HEADS + h)),
            pl.BlockSpec((ATTN_TILE, HEAD_DIM),
                         lambda b, h, qi, ki: (b * nq + jnp.minimum(ki, qi), 2 * N_SELF_HEADS + h)),
            pl.BlockSpec((None, None, 1, ATTN_TILE),
                         lambda b, h, qi, ki: (b, h, 0, jnp.minimum(ki, qi))),
        ],
        out_specs=pl.BlockSpec((ATTN_TILE, HEAD_DIM), lambda b, h, qi, ki: (b * nq + qi, h)),
        scratch_shapes=[pltpu.VMEM((ATTN_TILE, 1), F32), pltpu.VMEM((ATTN_TILE, 1), F32),
                        pltpu.VMEM((ATTN_TILE, HEAD_DIM), F32)],
        compiler_params=_params("arbitrary", "arbitrary", "arbitrary", "arbitrary"),
        name="fox_prompt")(qkv, qkv, qkv, ck)


def _fox_decode_kernel(pt_ref, q_ref, *refs):
    pp = PAGES_PER_STEP
    k_refs, v_refs, lf_refs = refs[:pp], refs[pp:2 * pp], refs[2 * pp:3 * pp]
    knew_ref, vnew_ref, lfnew_ref, o_ref, m_ref, l_ref, acc_ref, carry_ref = refs[3 * pp:]
    step = pl.program_id(1)
    rows = q_ref.shape[0]

    @pl.when(step == 0)
    def _():
        m_ref[...] = jnp.full(m_ref.shape, NEG_INF, F32)
        l_ref[...] = jnp.zeros(l_ref.shape, F32)
        acc_ref[...] = jnp.zeros(acc_ref.shape, F32)
        carry_ref[...] = jnp.zeros(carry_ref.shape, F32)

    def block(k, v, lf, mask):
        lf_rep = jnp.concatenate(
            [jnp.broadcast_to(lf[h:h + 1, :], (SUBLANES, LANES)) for h in range(N_SELF_HEADS)], axis=0)
        c = _scan_lanes(lf_rep) + carry_ref[...]
        carry_ref[...] = jnp.broadcast_to(c[:, LANES - 1:LANES], c.shape)
        s = _dot_nt(q_ref[...], k.astype(BF16)) * ATTN_SCALE - c
        if mask is not None:
            s = jnp.where(mask, s, NEG_INF)
        m_prev = m_ref[...]
        m_new = jnp.maximum(m_prev, jnp.max(s, axis=1, keepdims=True))
        alpha = jnp.exp(m_prev - m_new)
        p = jnp.exp(s - m_new)
        l_ref[...] = alpha * l_ref[...] + jnp.sum(p, axis=1, keepdims=True)
        acc_ref[...] = alpha * acc_ref[...] + _dot(p.astype(BF16), v.astype(BF16))
        m_ref[...] = m_new

    for j in range(pp):
        block(k_refs[j][...], v_refs[j][...], lf_refs[j][...], None)

    @pl.when(step == pl.num_programs(1) - 1)
    def _():
        n_new = o_ref.shape[0]
        i = lax.broadcasted_iota(jnp.int32, (rows, LANES), 0) & (SUBLANES - 1)
        col = lax.broadcasted_iota(jnp.int32, (rows, LANES), 1)
        block(knew_ref[...], vnew_ref[...], lfnew_ref[...], (col <= i) & (col < n_new))
        for h in range(N_SELF_HEADS):
            rs = slice(h * SUBLANES, (h + 1) * SUBLANES)
            cs = slice(h * HEAD_DIM, (h + 1) * HEAD_DIM)
            o_ref[:, cs] = (acc_ref[rs, cs] / l_ref[rs, :])[:n_new, :]


def _fox_decode(page_table, qbd, kc, vc, lfc, layer, knew, vnew, lfnew, n_new):
    db, n_pages = page_table.shape
    pp = PAGES_PER_STEP
    page, width = kc.shape[2], kc.shape[3]
    rows = qbd.shape[1]

    def cache_spec(j, shape):
        return pl.BlockSpec((None, None) + shape, lambda b, s, pt: (layer, pt[b, s * pp + j], 0, 0))

    per_b3 = lambda b, s, pt: (b, 0, 0)
    in_specs = [pl.BlockSpec((None, rows, width), per_b3)]
    in_specs += [cache_spec(j, (page, width)) for j in range(pp)]
    in_specs += [cache_spec(j, (page, width)) for j in range(pp)]
    in_specs += [cache_spec(j, (N_SELF_HEADS, page)) for j in range(pp)]
    in_specs += [pl.BlockSpec((None, page, width), per_b3), pl.BlockSpec((None, page, width), per_b3),
                 pl.BlockSpec((None, N_SELF_HEADS, page), per_b3)]
    grid_spec = pltpu.PrefetchScalarGridSpec(
        num_scalar_prefetch=1, grid=(db, n_pages // pp), in_specs=in_specs,
        out_specs=pl.BlockSpec((None, n_new, width), per_b3),
        scratch_shapes=[pltpu.VMEM((rows, 1), F32), pltpu.VMEM((rows, 1), F32),
                        pltpu.VMEM((rows, width), F32), pltpu.VMEM((rows, LANES), F32)])
    return pl.pallas_call(
        _fox_decode_kernel,
        out_shape=jax.ShapeDtypeStruct((db, n_new, width), F32),
        grid_spec=grid_spec,
        compiler_params=_params("arbitrary", "arbitrary"),
        name="fox_decode")(page_table, qbd, *([kc] * pp), *([vc] * pp), *([lfc] * pp), knew, vnew, lfnew)


def _banded_block(q, k_cur, v_cur, k_prev, v_prev, slope_d):
    n = q.shape[0]
    row = lax.broadcasted_iota(jnp.int32, (n, n), 0)
    col = lax.broadcasted_iota(jnp.int32, (n, n), 1)
    delta = (row - col).astype(F32)
    s_cur = _dot_nt(q, k_cur) * ATTN_SCALE - slope_d * delta
    s_cur = jnp.where(col <= row, s_cur, NEG_INF)
    m = jnp.max(s_cur, axis=1, keepdims=True)
    if k_prev is not None:
        s_prev = _dot_nt(q, k_prev) * ATTN_SCALE - slope_d * (delta + float(n))
        s_prev = jnp.where(col >= row, s_prev, NEG_INF)
        m = jnp.maximum(m, jnp.max(s_prev, axis=1, keepdims=True))
    e_cur = jnp.exp(s_cur - m)
    l = jnp.sum(e_cur, axis=1, keepdims=True)
    if k_prev is not None:
        e_prev = jnp.exp(s_prev - m)
        l = l + jnp.sum(e_prev, axis=1, keepdims=True)
    o = _dot((e_cur / l).astype(BF16), v_cur)
    if k_prev is not None:
        o = o + _dot((e_prev / l).astype(BF16), v_prev)
    return o, m + jnp.log(l)


def _combine_groups(outs, lses):
    mx = functools.reduce(jnp.maximum, lses)
    es = [jnp.exp(l - mx) for l in lses]
    den = functools.reduce(lambda a, b: a + b, es)
    return [o * (e / den) for o, e in zip(outs, es)]


def _dil_prompt_kernel(slopes_ref, *refs):
    ng = len(DIL_GROUPS)
    qkv_refs = refs[:3 * ng]
    o_refs = refs[3 * ng:4 * ng]
    qr, kr, vr, ores, lres = refs[4 * ng:4 * ng + 5]
    opos = refs[4 * ng + 5:5 * ng + 5]
    lpos = refs[5 * ng + 5:6 * ng + 5]
    slot = pl.program_id(1)
    seq = qr.shape[0]
    blk = LANES
    for g, (w, d) in enumerate(DIL_GROUPS):
        q_ref, k_ref, v_ref = qkv_refs[3 * g:3 * g + 3]
        seg = seq // d
        slope_d = slopes_ref[g * HEADS_PER_GROUP + slot] * float(d)
        for r in range(d):
            src = pl.ds(r, seg, stride=d) if d > 1 else pl.ds(0, seg)
            dst = pl.ds(r * seg, seg)
            qr[dst, :] = q_ref[src, :].astype(BF16)
            kr[dst, :] = k_ref[src, :].astype(BF16)
            vr[dst, :] = v_ref[src, :].astype(BF16)
        for qb in range(seq // blk):
            cur = pl.ds(qb * blk, blk)
            has_prev = (qb * blk) % seg != 0
            prev = pl.ds((qb - 1) * blk, blk)
            o, lse = _banded_block(qr[cur, :], kr[cur, :], vr[cur, :],
                                   kr[prev, :] if has_prev else None,
                                   vr[prev, :] if has_prev else None, slope_d)
            ores[cur, :] = o
            lres[cur, :] = jnp.broadcast_to(lse, (blk, LANES))
        for r in range(d):
            dst = pl.ds(r, seg, stride=d) if d > 1 else pl.ds(0, seg)
            src = pl.ds(r * seg, seg)
            opos[g][dst, :] = ores[src, :]
            lpos[g][dst, :] = lres[src, :]
    chunk = 256
    for c in range(seq // chunk):
        rows = pl.ds(c * chunk, chunk)
        outs = _combine_groups([opos[g][rows, :] for g in range(ng)], [lpos[g][rows, :] for g in range(ng)])
        for g in range(ng):
            o_refs[g][rows, :] = outs[g].astype(o_refs[g].dtype)


def _dil_prompt(qkv, slopes, n_rows, batch, seq):
    ng = len(DIL_GROUPS)
    for w, d in DIL_GROUPS:
        assert w // d == LANES and seq % (d * LANES) == 0
    in_specs = [pl.BlockSpec(memory_space=pltpu.SMEM)]
    for g in range(ng):
        for part in range(3):
            in_specs.append(pl.BlockSpec(
                (seq, HEAD_DIM),
                lambda b, s, g=g, part=part: (b, part * N_SELF_HEADS + g * HEADS_PER_GROUP + s)))
    out_spec = pl.BlockSpec((seq, HEAD_DIM), lambda b, s: (b, s))
    scratch = [pltpu.VMEM((seq, HEAD_DIM), BF16)] * 3 + [pltpu.VMEM((seq, HEAD_DIM), F32)] * (2 + 2 * ng)
    return pl.pallas_call(
        _dil_prompt_kernel,
        out_shape=tuple(jax.ShapeDtypeStruct((n_rows, GROUP_W), BF16) for _ in range(ng)),
        grid=(batch, HEADS_PER_GROUP),
        in_specs=in_specs,
        out_specs=tuple(out_spec for _ in range(ng)),
        scratch_shapes=scratch,
        compiler_params=_params("arbitrary", "arbitrary"),
        name="dilated_prompt")(slopes, *([qkv] * (3 * ng)))


def _dil_sample_kernel(slopes_ref, *refs, n_new, past_len):
    ng = len(DIL_GROUPS)
    qkv_refs = refs[:3 * ng]
    bk_ref, bv_ref = refs[3 * ng:3 * ng + 2]
    o_refs = refs[3 * ng + 2:4 * ng + 2]
    nk_ref, nv_ref = refs[4 * ng + 2:]
    slot = pl.program_id(1)
    nq = qkv_refs[0].shape[0]
    outs, lses = [], []
    off = 0
    for g, (w, d) in enumerate(DIL_GROUPS):
        assert d & (d - 1) == 0
        q_ref, k_ref, v_ref = qkv_refs[3 * g:3 * g + 3]
        slope = slopes_ref[g * HEADS_PER_GROUP + slot]
        row = lax.broadcasted_iota(jnp.int32, (nq, w), 0)
        col = lax.broadcasted_iota(jnp.int32, (nq, w), 1)
        dist = w + row - col
        ok = (dist <= w) & ((dist & (d - 1)) == 0) & (dist <= past_len + row)
        row2 = lax.broadcasted_iota(jnp.int32, (nq, nq), 0)
        col2 = lax.broadcasted_iota(jnp.int32, (nq, nq), 1)
        dist2 = row2 - col2
        ok2 = (dist2 >= 0) & ((dist2 & (d - 1)) == 0) & (col2 < n_new)
        q = q_ref[...].astype(BF16)
        kn = k_ref[...].astype(BF16)
        vn = v_ref[...].astype(BF16)
        kb = bk_ref[off:off + w, :].astype(BF16)
        vb = bv_ref[off:off + w, :].astype(BF16)
        s1 = _dot_nt(q, kb) * ATTN_SCALE - slope * dist.astype(F32)
        s1 = jnp.where(ok, s1, NEG_INF)
        s2 = _dot_nt(q, kn) * ATTN_SCALE - slope * dist2.astype(F32)
        s2 = jnp.where(ok2, s2, NEG_INF)
        m = jnp.maximum(jnp.max(s1, axis=1, keepdims=True), jnp.max(s2, axis=1, keepdims=True))
        e1 = jnp.exp(s1 - m)
        e2 = jnp.exp(s2 - m)
        l = jnp.sum(e1, axis=1, keepdims=True) + jnp.sum(e2, axis=1, keepdims=True)
        outs.append(_dot((e1 / l).astype(BF16), vb) + _dot((e2 / l).astype(BF16), vn))
        lses.append(m + jnp.log(l))
        keep = w - n_new
        step = 256
        for src_ref, new_ref, dst_ref in ((bk_ref, k_ref, nk_ref), (bv_ref, v_ref, nv_ref)):
            for c0 in range(0, keep, step):
                cn = min(step, keep - c0)
                dst_ref[off + c0:off + c0 + cn, :] = src_ref[off + n_new + c0:off + n_new + c0 + cn, :]
            dst_ref[off + keep:off + w, :] = new_ref[0:n_new, :]
        off += w
    mixed = _combine_groups(outs, lses)
    for g in range(ng):
        o_refs[g][...] = mixed[g]


def _dil_sample(slopes, qs, ks, vs, buf_k, buf_v, layer, n_new, past_len):
    ng = len(DIL_GROUPS)
    db, nq, _ = qs.shape
    rows = buf_k.shape[2]
    in_specs = [pl.BlockSpec(memory_space=pltpu.SMEM)]
    args = [slopes]
    for g in range(ng):
        for part, arr in enumerate((qs, ks, vs)):
            in_specs.append(pl.BlockSpec((None, nq, HEAD_DIM), lambda b, s, g=g: (b, 0, g * HEADS_PER_GROUP + s)))
            args.append(arr)
    buf_spec = pl.BlockSpec((None, None, rows, HEAD_DIM), lambda b, s: (layer, b, 0, s))
    in_specs += [buf_spec, buf_spec]
    args += [buf_k, buf_v]
    o_spec = pl.BlockSpec((None, nq, HEAD_DIM), lambda b, s: (b, 0, s))
    new_spec = pl.BlockSpec((None, rows, HEAD_DIM), lambda b, s: (b, 0, s))
    res = pl.pallas_call(
        functools.partial(_dil_sample_kernel, n_new=n_new, past_len=past_len),
        out_shape=tuple([jax.ShapeDtypeStruct((db, nq, GROUP_W), F32)] * ng
                        + [jax.ShapeDtypeStruct((db, rows, GROUP_W), F32)] * 2),
        grid=(db, HEADS_PER_GROUP),
        in_specs=in_specs,
        out_specs=tuple([o_spec] * ng + [new_spec] * 2),
        compiler_params=_params("arbitrary", "arbitrary"),
        name="dilated_sample")(*args)
    return res[:ng], res[ng], res[ng + 1]


def _mem_attn_kernel(q_ref, k_ref, v_ref, o_ref):
    s = _dot_nt(q_ref[...].astype(BF16), k_ref[...].astype(BF16)) * ATTN_SCALE
    m = jnp.max(s, axis=1, keepdims=True)
    e = jnp.exp(s - m)
    p = e / jnp.sum(e, axis=1, keepdims=True)
    o_ref[...] = _dot(p.astype(BF16), v_ref[...].astype(BF16)).astype(o_ref.dtype)


def _mem_attn(q, q_col0, mk, mv, layer, batch, rows_per_batch, q_tile, n_rows, out_dtype):
    nq = rows_per_batch // q_tile
    n_mem = mk.shape[2]
    kv_spec = pl.BlockSpec((None, None, n_mem, HEAD_DIM), lambda b, h, i: (layer, b, 0, h))
    return pl.pallas_call(
        _mem_attn_kernel,
        out_shape=jax.ShapeDtypeStruct((n_rows, MEM_W), out_dtype),
        grid=(batch, N_MEM_HEADS, nq),
        in_specs=[pl.BlockSpec((q_tile, HEAD_DIM), lambda b, h, i: (b * nq + i, q_col0 + h)), kv_spec, kv_spec],
        out_specs=pl.BlockSpec((q_tile, HEAD_DIM), lambda b, h, i: (b * nq + i, h)),
        compiler_params=_params("arbitrary", "arbitrary", "arbitrary"),
        name="mem_attn")(q, mk, mv)


def _oproj_kernel(x0, x1, x2, x3, w0, w1, w2, w3, y_ref, g_ref, b_ref, rwh_ref, rwl_ref, rb_ref,
                  y1_ref, y16_ref, lg_ref, *, alpha):
    h = _dot(x0[...], w0[...]) + _dot(x1[...], w1[...]) + _dot(x2[...], w2[...]) + _dot(x3[...], w3[...])
    y = _layer_norm(alpha * y_ref[...] + h, g_ref[...], b_ref[...])
    y1_ref[...] = y
    hi = y.astype(BF16)
    y16_ref[...] = hi
    lo = (y - hi.astype(F32)).astype(BF16)
    lg_ref[...] = _dot(hi, rwh_ref[...]) + (_dot(lo, rwh_ref[...]) + _dot(hi, rwl_ref[...])) + rb_ref[...]


def _oproj(xs, x_cols, w_out16, layer, y, g, b, rw_hi, rw_lo, rb, alpha):
    t, d = y.shape
    tm = 256
    x_specs = [pl.BlockSpec((tm, GROUP_W), lambda i, c=c: (i, c)) for c in x_cols]
    w_specs = [pl.BlockSpec((None, None, GROUP_W, d), lambda i, c=c: (layer, c, 0, 0)) for c in range(4)]
    row = pl.BlockSpec((tm, d), lambda i: (i, 0))
    vec = pl.BlockSpec((1, d), lambda i: (0, 0))
    rw = pl.BlockSpec((d, LANES), lambda i: (0, 0))
    lg = pl.BlockSpec((tm, LANES), lambda i: (i, 0))
    return pl.pallas_call(
        functools.partial(_oproj_kernel, alpha=alpha),
        out_shape=(jax.ShapeDtypeStruct((t, d), F32), jax.ShapeDtypeStruct((t, d), BF16),
                   jax.ShapeDtypeStruct((t, LANES), F32)),
        grid=(t // tm,),
        in_specs=x_specs + w_specs + [row, vec, vec, rw, rw, pl.BlockSpec((1, LANES), lambda i: (0, 0))],
        out_specs=(row, row, lg),
        compiler_params=_params("arbitrary"),
        name="out_proj_norm_router")(*xs, w_out16, w_out16, w_out16, w_out16, y, g, b, rw_hi, rw_lo, rb)


def _ln2_kernel(y_ref, f_ref, g_ref, b_ref, o_ref, o16_ref, *, alpha):
    y = _layer_norm(alpha * y_ref[...] + f_ref[...], g_ref[...], b_ref[...])
    o_ref[...] = y
    o16_ref[...] = y.astype(BF16)


def _ln2(y, f, g, b, alpha):
    t, d = y.shape
    tm = 256
    row = pl.BlockSpec((tm, d), lambda i: (i, 0))
    vec = pl.BlockSpec((1, d), lambda i: (0, 0))
    return pl.pallas_call(
        functools.partial(_ln2_kernel, alpha=alpha),
        out_shape=(jax.ShapeDtypeStruct((t, d), F32), jax.ShapeDtypeStruct((t, d), BF16)),
        grid=(t // tm,),
        in_specs=[row, row, vec, vec],
        out_specs=(row, row),
        compiler_params=_params("arbitrary"),
        name="moe_residual_norm")(y, f, g, b)


def _moe_up_kernel(te_ref, tv_ref, nw_ref, x_ref, wg_ref, wl_ref, bg_ref, bl_ref, o_ref, wgb_ref, wlb_ref):
    t = pl.program_id(1)

    @pl.when(nw_ref[t] == 1)
    def _():
        wgb_ref[...] = wg_ref[...].astype(BF16)
        wlb_ref[...] = wl_ref[...].astype(BF16)

    @pl.when(tv_ref[t] == 1)
    def _():
        x = x_ref[...]
        hg = _dot(x, wgb_ref[...]) + bg_ref[...]
        hl = _dot(x, wlb_ref[...]) + bl_ref[...]
        gate = jnp.minimum(hg, SWIGLU_LIMIT)
        lin = jnp.clip(hl, -SWIGLU_LIMIT, SWIGLU_LIMIT)
        act = gate * (1.0 / (1.0 + jnp.exp(-SWIGLU_ALPHA * gate))) * (lin + 1.0)
        o_ref[...] = act.astype(o_ref.dtype)

    @pl.when(tv_ref[t] == 0)
    def _():
        o_ref[...] = jnp.zeros(o_ref.shape, o_ref.dtype)


def _moe_up(tile_e, tile_valid, new_w, xs, w1, b1, layer):
    m, d = xs.shape
    f = w1.shape[3] // 2
    nj = f // COL_TILE
    grid_spec = pltpu.PrefetchScalarGridSpec(
        num_scalar_prefetch=3, grid=(nj, m // MOE_TILE),
        in_specs=[
            pl.BlockSpec((MOE_TILE, d), lambda j, t, te, tv, nw: (t, 0)),
            pl.BlockSpec((None, None, d, COL_TILE), lambda j, t, te, tv, nw: (layer, te[t], 0, j)),
            pl.BlockSpec((None, None, d, COL_TILE), lambda j, t, te, tv, nw: (layer, te[t], 0, nj + j)),
            pl.BlockSpec((None, None, 1, COL_TILE), lambda j, t, te, tv, nw: (layer, te[t], 0, j)),
            pl.BlockSpec((None, None, 1, COL_TILE), lambda j, t, te, tv, nw: (layer, te[t], 0, nj + j)),
        ],
        out_specs=pl.BlockSpec((MOE_TILE, COL_TILE), lambda j, t, te, tv, nw: (t, j)),
        scratch_shapes=[pltpu.VMEM((d, COL_TILE), BF16), pltpu.VMEM((d, COL_TILE), BF16)])
    return pl.pallas_call(
        _moe_up_kernel,
        out_shape=jax.ShapeDtypeStruct((m, f), BF16),
        grid_spec=grid_spec,
        compiler_params=_params("arbitrary", "arbitrary"),
        name="moe_up")(tile_e, tile_valid, new_w, xs, w1, w1, b1, b1)


def _moe_down_kernel(te_ref, tv_ref, nw_ref, a_ref, w_ref, b_ref, o_ref, wb_ref):
    t = pl.program_id(1)

    @pl.when(nw_ref[t] == 1)
    def _():
        wb_ref[...] = w_ref[...].astype(BF16)

    @pl.when(tv_ref[t] == 1)
    def _():
        o_ref[...] = _dot(a_ref[...], wb_ref[...]) + b_ref[...]

    @pl.when(tv_ref[t] == 0)
    def _():
        o_ref[...] = jnp.zeros(o_ref.shape, o_ref.dtype)


def _moe_down(tile_e, tile_valid, new_w, act, w2, b2, layer):
    m, f = act.shape
    d = w2.shape[3]
    grid_spec = pltpu.PrefetchScalarGridSpec(
        num_scalar_prefetch=3, grid=(d // COL_TILE, m // MOE_TILE),
        in_specs=[
            pl.BlockSpec((MOE_TILE, f), lambda j, t, te, tv, nw: (t, 0)),
            pl.BlockSpec((None, None, f, COL_TILE), lambda j, t, te, tv, nw: (layer, te[t], 0, j)),
            pl.BlockSpec((None, None, 1, COL_TILE), lambda j, t, te, tv, nw: (layer, te[t], 0, j)),
        ],
        out_specs=pl.BlockSpec((MOE_TILE, COL_TILE), lambda j, t, te, tv, nw: (t, j)),
        scratch_shapes=[pltpu.VMEM((f, COL_TILE), BF16)])
    return pl.pallas_call(
        _moe_down_kernel,
        out_shape=jax.ShapeDtypeStruct((m, d), F32),
        grid_spec=grid_spec,
        compiler_params=_params("arbitrary", "arbitrary"),
        name="moe_down")(tile_e, tile_valid, new_w, act, w2, b2)


def _route(logits):
    t = logits.shape[0]
    top_val, top_idx = lax.top_k(logits, TOP_K)
    gates = jax.nn.softmax(top_val, axis=-1)
    a = t * TOP_K
    e_flat = top_idx.reshape(a).astype(jnp.int32)
    onehot = (e_flat[:, None] == jnp.arange(N_EXPERTS, dtype=jnp.int32)[None, :]).astype(jnp.int32)
    csum = jnp.cumsum(onehot, axis=0)
    counts = csum[-1]
    rank = jnp.take_along_axis(csum, e_flat[:, None], axis=1)[:, 0] - 1
    padded = (counts + MOE_TILE - 1) // MOE_TILE * MOE_TILE
    pend = jnp.cumsum(padded)
    pstart = pend - padded
    pos = pstart[e_flat] + rank
    n_tiles = -(-(a + N_EXPERTS * (MOE_TILE - 1)) // MOE_TILE)
    tile_start = jnp.arange(n_tiles, dtype=jnp.int32) * MOE_TILE
    valid = tile_start < pend[-1]
    raw = jnp.minimum(jnp.searchsorted(pend, tile_start, side="right"), N_EXPERTS - 1).astype(jnp.int32)
    tile_e = jnp.where(valid, raw, jnp.max(jnp.where(valid, raw, 0)))
    new_w = jnp.concatenate([jnp.ones((1,), jnp.int32), (tile_e[1:] != tile_e[:-1]).astype(jnp.int32)])
    tok_sorted = jnp.zeros((n_tiles * MOE_TILE,), jnp.int32).at[pos].set(
        jnp.arange(a, dtype=jnp.int32) // TOP_K)
    return gates, pos.reshape(t, TOP_K), tok_sorted, tile_e, valid.astype(jnp.int32), new_w


def _alibi_slopes(n):
    return 2.0 ** (-8.0 * jnp.arange(1, n + 1, dtype=F32) / n)


def kernel(x_prompt, x_sample, cache_fox_k, cache_fox_v, cache_fox_logf, cache_win_k, cache_win_v,
           cache_mem_k, cache_mem_v, page_table, mem_prompt, w_in_a, b_fgate, w_in_b, w_mem_kv, w_out,
           ln1_g, ln1_b, router_w, router_b, w1, b1, w2, b2, ln2_g, ln2_b):
    B, S, D = x_prompt.shape
    DB, n, _ = x_sample.shape
    depth = w_out.shape[0]
    n_mem = mem_prompt.shape[1]
    n_prompt = B * S
    n_tok = n_prompt + DB * n
    t_pad = -(-n_tok // ROW_TILE) * ROW_TILE
    s_blk = n_prompt // ROW_TILE
    assert n_prompt % ROW_TILE == 0 and DB * n <= ROW_TILE and n <= SUBLANES
    alpha = float((2 * depth) ** 0.25)
    slopes = _alibi_slopes(N_SELF_HEADS)
    n_pool, page = cache_fox_k.shape[1], cache_fox_k.shape[2]

    tok = jnp.concatenate([x_prompt.reshape(n_prompt, D), x_sample.reshape(DB * n, D),
                           jnp.zeros((t_pad - n_tok, D), F32)], axis=0)
    tok16 = tok.astype(BF16)

    mem_k_p, mem_v_p = _mem_kv(mem_prompt.reshape(B * n_mem, D), w_mem_kv)
    mem_k_p = mem_k_p.reshape(depth, B, n_mem, MEM_W)
    mem_v_p = mem_v_p.reshape(depth, B, n_mem, MEM_W)
    cmk = cache_mem_k.reshape(depth, DB, n_mem, MEM_W)
    cmv = cache_mem_v.reshape(depth, DB, n_mem, MEM_W)

    kc = cache_fox_k.reshape(cache_fox_k.shape[0], n_pool, page, SELF_W)
    vc = cache_fox_v.reshape(cache_fox_v.shape[0], n_pool, page, SELF_W)
    lfc = cache_fox_logf.transpose(0, 1, 3, 2)
    win_rows = cache_win_k.shape[2]
    wk = cache_win_k.reshape(cache_win_k.shape[0], DB, win_rows, GROUP_W)
    wv = cache_win_v.reshape(cache_win_v.shape[0], DB, win_rows, GROUP_W)

    w_out16 = w_out.astype(BF16).reshape(depth, 4, GROUP_W, D)
    b1r = b1.reshape(b1.shape[0], b1.shape[1], 1, b1.shape[2])
    b2r = b2.reshape(b2.shape[0], b2.shape[1], 1, b2.shape[2])
    eye = jnp.eye(N_SELF_HEADS, dtype=F32)

    def sample_rows(arr, c0, width):
        r = arr[n_prompt:n_tok, c0:c0 + width].reshape(DB, n, width)
        return jnp.pad(r, ((0, 0), (0, SUBLANES - n), (0, 0)))

    def put_sample(table, rows):
        width = table.shape[1]
        blk = jnp.zeros((ROW_TILE, width), table.dtype).at[:DB * n].set(
            rows.reshape(DB * n, width).astype(table.dtype))
        return lax.dynamic_update_slice(table, blk, (n_prompt, 0))

    fk_p, fv_p, flf_p, fk_s, fv_s, flf_s = [], [], [], [], [], []
    wk_p, wv_p, wk_s, wv_s = [], [], [], []
    for i in range(depth):
        j = i // N_MIXERS
        if i % N_MIXERS == 0:
            qkv = _proj(tok16, w_in_a, j, 3 * SELF_W)
            w_f = w_in_a[j][:, 3 * SELF_W:3 * SELF_W + N_SELF_HEADS]
            w_qm = w_in_a[j][:, 3 * SELF_W + N_SELF_HEADS:]
            qm = _proj(tok16, w_qm[None], 0, MEM_W)
            qm_col0 = 0
            hp = 2 * SUBLANES
            wf_t = jnp.pad(w_f.T, ((0, hp - N_SELF_HEADS), (0, 0))).astype(BF16)
            b_col = jnp.pad(b_fgate[j], (0, hp - N_SELF_HEADS)).reshape(hp, 1)
            lf_p, c_p = _gate(tok16, wf_t, b_col, S, 0, B)
            lf_s, _ = _gate(tok16, wf_t, b_col, ROW_TILE, s_blk, 1)
            ck = c_p[:, :N_SELF_HEADS].reshape(B, N_SELF_HEADS, 1, S)
            o_self = _fox_prompt(qkv, ck, t_pad, B, S)
            q_s = sample_rows(qkv, 0, SELF_W)
            k_s = sample_rows(qkv, SELF_W, SELF_W)
            v_s = sample_rows(qkv, 2 * SELF_W, SELF_W)
            lf_s = lf_s[0, :N_SELF_HEADS, :DB * n].reshape(N_SELF_HEADS, DB, n).transpose(1, 0, 2)
            qh = q_s.reshape(DB, SUBLANES, N_SELF_HEADS, HEAD_DIM).transpose(0, 2, 1, 3)
            qbd = (qh[:, :, :, None, :] * eye[None, :, None, :, None]).reshape(
                DB, N_SELF_HEADS * SUBLANES, SELF_W).astype(BF16)
            knew = jnp.pad(k_s, ((0, 0), (0, page - SUBLANES), (0, 0)))
            vnew = jnp.pad(v_s, ((0, 0), (0, page - SUBLANES), (0, 0)))
            lfnew = jnp.pad(lf_s, ((0, 0), (0, 0), (0, page - n)))
            o_s = _fox_decode(page_table, qbd, kc, vc, lfc, j, knew, vnew, lfnew, n)
            o_self = put_sample(o_self, o_s)
            xs_parts, x_cols = [o_self, o_self, o_self], [0, 1, 2]
            fk_p.append(qkv[:n_prompt, SELF_W:2 * SELF_W].reshape(B, S, N_SELF_HEADS, HEAD_DIM))
            fv_p.append(qkv[:n_prompt, 2 * SELF_W:3 * SELF_W].reshape(B, S, N_SELF_HEADS, HEAD_DIM))
            flf_p.append(lf_p[:, :N_SELF_HEADS].transpose(0, 2, 1))
            fk_s.append(k_s[:, :n].reshape(DB, n, N_SELF_HEADS, HEAD_DIM))
            fv_s.append(v_s[:, :n].reshape(DB, n, N_SELF_HEADS, HEAD_DIM))
            flf_s.append(lf_s.transpose(0, 2, 1))
        else:
            qkv = _proj(tok16, w_in_b, j, 3 * SELF_W + MEM_W)
            qm, qm_col0 = qkv, 3 * N_SELF_HEADS
            o_groups = _dil_prompt(qkv, slopes, t_pad, B, S)
            q_s = sample_rows(qkv, 0, SELF_W)
            k_s = sample_rows(qkv, SELF_W, SELF_W)
            v_s = sample_rows(qkv, 2 * SELF_W, SELF_W)
            o_s, nk, nv = _dil_sample(slopes, q_s, k_s, v_s, wk, wv, j, n, page_table.shape[1] * page)
            xs_parts = [put_sample(o_groups[g], o_s[g][:, :n]) for g in range(len(DIL_GROUPS))]
            x_cols = [0, 0, 0]
            kp = qkv[:n_prompt, SELF_W:2 * SELF_W].reshape(B, S, N_SELF_HEADS, HEAD_DIM)
            vp = qkv[:n_prompt, 2 * SELF_W:3 * SELF_W].reshape(B, S, N_SELF_HEADS, HEAD_DIM)
            bk, bv = [], []
            for g, (w, d) in enumerate(DIL_GROUPS):
                hs = slice(g * HEADS_PER_GROUP, (g + 1) * HEADS_PER_GROUP)
                pad = ((0, 0), (w, 0), (0, 0), (0, 0))
                bk.append(jnp.pad(kp[:, :, hs], pad)[:, -w:])
                bv.append(jnp.pad(vp[:, :, hs], pad)[:, -w:])
            wk_p.append(jnp.concatenate(bk, axis=1))
            wv_p.append(jnp.concatenate(bv, axis=1))
            wk_s.append(nk.reshape(DB, win_rows, HEADS_PER_GROUP, HEAD_DIM))
            wv_s.append(nv.reshape(DB, win_rows, HEADS_PER_GROUP, HEAD_DIM))
        o_mem = _mem_attn(qm, qm_col0, mem_k_p, mem_v_p, i, B, S, ATTN_TILE, t_pad, BF16)
        qm_s = sample_rows(qm, qm_col0 * HEAD_DIM, MEM_W).reshape(DB * SUBLANES, MEM_W)
        om_s = _mem_attn(qm_s, 0, cmk, cmv, i, DB, SUBLANES, SUBLANES, DB * SUBLANES, F32)
        o_mem = put_sample(o_mem, om_s.reshape(DB, SUBLANES, MEM_W)[:, :n])
        rw = jnp.pad(router_w[i], ((0, 0), (0, LANES - N_EXPERTS)))
        rw_hi = rw.astype(BF16)
        rw_lo = (rw - rw_hi.astype(F32)).astype(BF16)
        rb = jnp.pad(router_b[i], (0, LANES - N_EXPERTS)).reshape(1, LANES)
        y1, y16, logits = _oproj(xs_parts + [o_mem], x_cols + [0], w_out16, i, tok,
                                 ln1_g[i].reshape(1, D), ln1_b[i].reshape(1, D), rw_hi, rw_lo, rb, alpha)
        gates, pos, tok_sorted, tile_e, tile_valid, new_w = _route(logits[:n_tok, :N_EXPERTS])
        xs = jnp.take(y16, tok_sorted, axis=0)
        act = _moe_up(tile_e, tile_valid, new_w, xs, w1, b1r, i)
        out_sorted = _moe_down(tile_e, tile_valid, new_w, act, w2, b2r, i)
        f = jnp.sum(jnp.take(out_sorted, pos, axis=0) * gates[:, :, None], axis=1)
        f = jnp.pad(f, ((0, t_pad - n_tok), (0, 0)))
        tok, tok16 = _ln2(y1, f, ln2_g[i].reshape(1, D), ln2_b[i].reshape(1, D), alpha)

    yp = tok[:n_prompt].reshape(B, S, D)
    ys = tok[n_prompt:n_tok].reshape(DB, n, D)
    mk_out = mem_k_p.reshape(depth, B, n_mem, N_MEM_HEADS, HEAD_DIM)
    mv_out = mem_v_p.reshape(depth, B, n_mem, N_MEM_HEADS, HEAD_DIM)
    return (yp, ys, jnp.stack(fk_p), jnp.stack(fv_p), jnp.stack(flf_p), jnp.stack(wk_p), jnp.stack(wv_p),
            mk_out, mv_out, jnp.stack(fk_s), jnp.stack(fv_s), jnp.stack(flf_s), jnp.stack(wk_s), jnp.stack(wv_s))
```

```python
import functools

import numpy as np
import jax
import jax.numpy as jnp
from jax import lax
from jax.experimental import pallas as pl
from jax.experimental.pallas import tpu as pltpu

HEAD_DIM = 128
N_SELF_HEADS = 12
N_MEM_HEADS = 4
DIL_GROUPS = ((128, 1), (512, 4), (2048, 16))
HEADS_PER_GROUP = N_SELF_HEADS // len(DIL_GROUPS)
N_EXPERTS = 32
TOP_K = 4
SWIGLU_ALPHA = 1.702
SWIGLU_LIMIT = 7.0
N_MIXERS = 2
LN_EPS = 1e-5
ATTN_SCALE = HEAD_DIM ** -0.5
SELF_W = N_SELF_HEADS * HEAD_DIM
MEM_W = N_MEM_HEADS * HEAD_DIM
GROUP_W = HEADS_PER_GROUP * HEAD_DIM

LANES = 128
SUBLANES = 8
ROW_TILE = 512
COL_TILE = 512
ATTN_TILE = 512
MOE_TILE = 256
MOE_UP_COLS = 1024
MOE_DOWN_COLS = 2048
TOKEN_TILE = 128
PAGES_PER_STEP = 8
HEAD_ROWS = 16
VMEM_LIMIT = 56 * 1024 * 1024

BF16 = jnp.bfloat16
F32 = jnp.float32
NEG_INF = float("-inf")


def _params(*sem):
    return pltpu.CompilerParams(dimension_semantics=sem, vmem_limit_bytes=VMEM_LIMIT)


def _dot(a, b):
    return jnp.dot(a, b, preferred_element_type=F32)


def _dot_nt(a, b):
    return lax.dot_general(a, b, (((1,), (1,)), ((), ())), preferred_element_type=F32)


def _scan_lanes(v):
    lane = lax.broadcasted_iota(jnp.int32, v.shape, 1)
    shift = 1
    while shift < LANES:
        v = v + jnp.where(lane >= shift, pltpu.roll(v, shift, axis=1), 0.0)
        shift *= 2
    return v


def _layer_norm(r, g, b):
    mu = jnp.mean(r, axis=-1, keepdims=True)
    xc = r - mu
    var = jnp.mean(xc * xc, axis=-1, keepdims=True)
    return xc * lax.rsqrt(var + LN_EPS) * g + b


def _proj_kernel(x_ref, w_ref, o_ref, wbf_ref):
    @pl.when(pl.program_id(1) == 0)
    def _():
        wbf_ref[...] = w_ref[...].astype(BF16)

    o_ref[...] = _dot(x_ref[...], wbf_ref[...])


def _proj(x16, w, layer, n_cols):
    t, k = x16.shape
    return pl.pallas_call(
        _proj_kernel,
        out_shape=jax.ShapeDtypeStruct((t, n_cols), F32),
        grid=(n_cols // COL_TILE, t // ROW_TILE),
        in_specs=[pl.BlockSpec((ROW_TILE, k), lambda j, i: (i, 0)),
                  pl.BlockSpec((None, k, COL_TILE), lambda j, i: (layer, 0, j))],
        out_specs=pl.BlockSpec((ROW_TILE, COL_TILE), lambda j, i: (i, j)),
        scratch_shapes=[pltpu.VMEM((k, COL_TILE), BF16)],
        compiler_params=_params("arbitrary", "arbitrary"),
        name="proj")(x16, w)


def _memkv_kernel(x_ref, w_ref, o_ref):
    o_ref[...] = _dot(x_ref[...].astype(BF16), w_ref[...].astype(BF16))


def _mem_kv(mem2d, w_mem_kv):
    r, d = mem2d.shape
    depth = w_mem_kv.shape[0]
    tm = min(ROW_TILE, r)
    assert r % tm == 0
    out = pl.pallas_call(
        _memkv_kernel,
        out_shape=jax.ShapeDtypeStruct((depth, 2, r, MEM_W), F32),
        grid=(depth, 2, r // tm),
        in_specs=[pl.BlockSpec((tm, d), lambda l, c, i: (i, 0)),
                  pl.BlockSpec((None, d, MEM_W), lambda l, c, i: (l, 0, c))],
        out_specs=pl.BlockSpec((None, None, tm, MEM_W), lambda l, c, i: (l, c, i, 0)),
        compiler_params=_params("arbitrary", "arbitrary", "arbitrary"),
        name="mem_kv")(mem2d, w_mem_kv)
    return out[:, 0], out[:, 1]


def _gate_kernel(x_ref, wf_ref, bf_ref, lf_ref, c_ref):
    z = _dot_nt(wf_ref[...], x_ref[...]) + bf_ref[...]
    lf = jnp.minimum(z, 0.0) - jnp.log1p(jnp.exp(-jnp.abs(z)))
    lf_ref[...] = lf
    carry = jnp.zeros((lf.shape[0], 1), F32)
    for blk in range(lf.shape[1] // LANES):
        c = _scan_lanes(lf[:, blk * LANES:(blk + 1) * LANES]) + carry
        c_ref[:, blk * LANES:(blk + 1) * LANES] = c
        carry = c[:, LANES - 1:LANES]


def _gate(x16, wf_t, b_col, blk_rows, first_block, n_blocks):
    k = x16.shape[1]
    hp = wf_t.shape[0]
    shp = jax.ShapeDtypeStruct((n_blocks, hp, blk_rows), F32)
    return pl.pallas_call(
        _gate_kernel,
        out_shape=(shp, shp),
        grid=(n_blocks,),
        in_specs=[pl.BlockSpec((blk_rows, k), lambda b: (first_block + b, 0)),
                  pl.BlockSpec((hp, k), lambda b: (0, 0)),
                  pl.BlockSpec((hp, 1), lambda b: (0, 0))],
        out_specs=(pl.BlockSpec((None, hp, blk_rows), lambda b: (b, 0, 0)),
                   pl.BlockSpec((None, hp, blk_rows), lambda b: (b, 0, 0))),
        compiler_params=_params("arbitrary"),
        name="forget_gate")(x16, wf_t, b_col)


def _fox_prompt_kernel(q_ref, k_ref, v_ref, ck_ref, o_ref, m_ref, l_ref, acc_ref):
    qi = pl.program_id(2)
    ki = pl.program_id(3)

    @pl.when(ki == 0)
    def _():
        m_ref[...] = jnp.full(m_ref.shape, NEG_INF, F32)
        l_ref[...] = jnp.zeros(l_ref.shape, F32)
        acc_ref[...] = jnp.zeros(acc_ref.shape, F32)

    def update(diagonal):
        s = _dot_nt(q_ref[...].astype(BF16), k_ref[...].astype(BF16)) * ATTN_SCALE
        s = s - ck_ref[...]
        if diagonal:
            row = lax.broadcasted_iota(jnp.int32, s.shape, 0)
            col = lax.broadcasted_iota(jnp.int32, s.shape, 1)
            s = jnp.where(col <= row, s, NEG_INF)
        m_prev = m_ref[...]
        m_new = jnp.maximum(m_prev, jnp.max(s, axis=1, keepdims=True))
        alpha = jnp.exp(m_prev - m_new)
        p = jnp.exp(s - m_new)
        l_ref[...] = alpha * l_ref[...] + jnp.sum(p, axis=1, keepdims=True)
        acc_ref[...] = alpha * acc_ref[...] + _dot(p.astype(BF16), v_ref[...].astype(BF16))
        m_ref[...] = m_new

    @pl.when(ki < qi)
    def _():
        update(False)

    @pl.when(ki == qi)
    def _():
        update(True)
        o_ref[...] = (acc_ref[...] / l_ref[...]).astype(o_ref.dtype)


def _fox_prompt(qkv, ck, n_rows, batch, seq):
    assert n_rows == batch * seq
    nq = seq // ATTN_TILE
    return pl.pallas_call(
        _fox_prompt_kernel,
        out_shape=jax.ShapeDtypeStruct((n_rows, SELF_W), BF16),
        grid=(batch, N_SELF_HEADS, nq, nq),
        in_specs=[
            pl.BlockSpec((ATTN_TILE, HEAD_DIM), lambda b, h, qi, ki: (b * nq + qi, h)),
            pl.BlockSpec((ATTN_TILE, HEAD_DIM),
                         lambda b, h, qi, ki: (b * nq + jnp.minimum(ki, qi), N_SELF_HEADS + h)),
            pl.BlockSpec((ATTN_TILE, HEAD_DIM),
                         lambda b, h, qi, ki: (b * nq + jnp.minimum(ki, qi), 2 * N_SELF_HEADS + h)),
            pl.BlockSpec((None, None, 1, ATTN_TILE),
                         lambda b, h, qi, ki: (b, h, 0, jnp.minimum(ki, qi))),
        ],
        out_specs=pl.BlockSpec((ATTN_TILE, HEAD_DIM), lambda b, h, qi, ki: (b * nq + qi, h)),
        scratch_shapes=[pltpu.VMEM((ATTN_TILE, 1), F32), pltpu.VMEM((ATTN_TILE, 1), F32),
                        pltpu.VMEM((ATTN_TILE, HEAD_DIM), F32)],
        compiler_params=_params("arbitrary", "arbitrary", "arbitrary", "arbitrary"),
        name="fox_prompt")(qkv, qkv, qkv, ck)


def _fox_decode_kernel(pt_ref, q_ref, *refs, n_new):
    pp = PAGES_PER_STEP
    k_refs, v_refs, lf_refs = refs[:pp], refs[pp:2 * pp], refs[2 * pp:3 * pp]
    knew_ref, vnew_ref, lfnew_ref, o_ref, m_ref, l_ref, acc_ref, carry_ref = refs[3 * pp:]
    step = pl.program_id(1)
    rows = q_ref.shape[0]
    hr = HEAD_ROWS

    @pl.when(step == 0)
    def _():
        m_ref[...] = jnp.full(m_ref.shape, NEG_INF, F32)
        l_ref[...] = jnp.zeros(l_ref.shape, F32)
        acc_ref[...] = jnp.zeros(acc_ref.shape, F32)
        carry_ref[...] = jnp.zeros(carry_ref.shape, F32)

    def block(k_ref, v_ref, lf, mask):
        lf_rep = jnp.concatenate(
            [jnp.broadcast_to(lf[h:h + 1, :], (hr, LANES)) for h in range(N_SELF_HEADS)], axis=0)
        c = _scan_lanes(lf_rep) + carry_ref[...]
        carry_ref[...] = jnp.broadcast_to(c[:, LANES - 1:LANES], c.shape)
        s = jnp.concatenate(
            [_dot_nt(q_ref[h * hr:(h + 1) * hr, :], k_ref[h].astype(BF16)) for h in range(N_SELF_HEADS)], axis=0)
        s = s * ATTN_SCALE - c
        if mask is not None:
            s = jnp.where(mask, s, NEG_INF)
        m_prev = m_ref[...]
        m_new = jnp.maximum(m_prev, jnp.max(s, axis=1, keepdims=True))
        alpha = jnp.exp(m_prev - m_new)
        p = jnp.exp(s - m_new)
        l_ref[...] = alpha * l_ref[...] + jnp.sum(p, axis=1, keepdims=True)
        p16 = p.astype(BF16)
        pv = jnp.concatenate(
            [_dot(p16[h * hr:(h + 1) * hr, :], v_ref[h].astype(BF16)) for h in range(N_SELF_HEADS)], axis=0)
        acc_ref[...] = alpha * acc_ref[...] + pv
        m_ref[...] = m_new

    for j in range(pp):
        block(k_refs[j], v_refs[j], lf_refs[j][...], None)

    @pl.when(step == pl.num_programs(1) - 1)
    def _():
        i = lax.broadcasted_iota(jnp.int32, (rows, LANES), 0) & (hr - 1)
        col = lax.broadcasted_iota(jnp.int32, (rows, LANES), 1)
        block(knew_ref, vnew_ref, lfnew_ref[...], (col <= i) & (col < n_new))
        o_ref[...] = acc_ref[...] / l_ref[...]


def _fox_decode(page_table, q16, kc, vc, lfc, layer, knew, vnew, lfnew, n_new):
    db, n_pages = page_table.shape
    pp = PAGES_PER_STEP
    assert n_pages % pp == 0 and HEAD_ROWS & (HEAD_ROWS - 1) == 0 and n_new <= HEAD_ROWS
    page = kc.shape[3]
    rows = q16.shape[1]

    def kv_spec(j):
        return pl.BlockSpec((None, None, N_SELF_HEADS, page, HEAD_DIM),
                            lambda b, s, pt: (layer, pt[b, s * pp + j], 0, 0, 0))

    def lf_spec(j):
        return pl.BlockSpec((None, None, N_SELF_HEADS, page), lambda b, s, pt: (layer, pt[b, s * pp + j], 0, 0))

    per_b3 = lambda b, s, pt: (b, 0, 0)
    per_b4 = lambda b, s, pt: (b, 0, 0, 0)
    in_specs = [pl.BlockSpec((None, rows, HEAD_DIM), per_b3)]
    in_specs += [kv_spec(j) for j in range(pp)]
    in_specs += [kv_spec(j) for j in range(pp)]
    in_specs += [lf_spec(j) for j in range(pp)]
    in_specs += [pl.BlockSpec((None, N_SELF_HEADS, page, HEAD_DIM), per_b4),
                 pl.BlockSpec((None, N_SELF_HEADS, page, HEAD_DIM), per_b4),
                 pl.BlockSpec((None, N_SELF_HEADS, page), per_b3)]
    grid_spec = pltpu.PrefetchScalarGridSpec(
        num_scalar_prefetch=1, grid=(db, n_pages // pp), in_specs=in_specs,
        out_specs=pl.BlockSpec((None, rows, HEAD_DIM), per_b3),
        scratch_shapes=[pltpu.VMEM((rows, 1), F32), pltpu.VMEM((rows, 1), F32),
                        pltpu.VMEM((rows, HEAD_DIM), F32), pltpu.VMEM((rows, LANES), F32)])
    return pl.pallas_call(
        functools.partial(_fox_decode_kernel, n_new=n_new),
        out_shape=jax.ShapeDtypeStruct((db, rows, HEAD_DIM), F32),
        grid_spec=grid_spec,
        compiler_params=_params("arbitrary", "arbitrary"),
        name="fox_decode")(page_table, q16, *([kc] * pp), *([vc] * pp), *([lfc] * pp), knew, vnew, lfnew)


def _banded_block(q, k_cur, v_cur, k_prev, v_prev, slope_d):
    n = q.shape[0]
    row = lax.broadcasted_iota(jnp.int32, (n, n), 0)
    col = lax.broadcasted_iota(jnp.int32, (n, n), 1)
    delta = (row - col).astype(F32)
    s_cur = _dot_nt(q, k_cur) * ATTN_SCALE - slope_d * delta
    s_cur = jnp.where(col <= row, s_cur, NEG_INF)
    m = jnp.max(s_cur, axis=1, keepdims=True)
    if k_prev is not None:
        s_prev = _dot_nt(q, k_prev) * ATTN_SCALE - slope_d * (delta + float(n))
        s_prev = jnp.where(col >= row, s_prev, NEG_INF)
        m = jnp.maximum(m, jnp.max(s_prev, axis=1, keepdims=True))
    e_cur = jnp.exp(s_cur - m)
    l = jnp.sum(e_cur, axis=1, keepdims=True)
    if k_prev is not None:
        e_prev = jnp.exp(s_prev - m)
        l = l + jnp.sum(e_prev, axis=1, keepdims=True)
    o = _dot((e_cur / l).astype(BF16), v_cur)
    if k_prev is not None:
        o = o + _dot((e_prev / l).astype(BF16), v_prev)
    return o, m + jnp.log(l)


def _combine_groups(outs, lses):
    mx = functools.reduce(jnp.maximum, lses)
    es = [jnp.exp(l - mx) for l in lses]
    den = functools.reduce(lambda a, b: a + b, es)
    return [o * (e / den) for o, e in zip(outs, es)]


def _dil_prompt_kernel(slopes_ref, *refs):
    ng = len(DIL_GROUPS)
    qkv_refs = refs[:3 * ng]
    o_refs = refs[3 * ng:4 * ng]
    qr, kr, vr, ores, lres = refs[4 * ng:4 * ng + 5]
    opos = refs[4 * ng + 5:5 * ng + 5]
    lpos = refs[5 * ng + 5:6 * ng + 5]
    slot = pl.program_id(1)
    seq = qr.shape[0]
    blk = LANES
    for g, (w, d) in enumerate(DIL_GROUPS):
        q_ref, k_ref, v_ref = qkv_refs[3 * g:3 * g + 3]
        seg = seq // d
        slope_d = slopes_ref[g * HEADS_PER_GROUP + slot] * float(d)
        for r in range(d):
            src = pl.ds(r, seg, stride=d) if d > 1 else pl.ds(0, seg)
            dst = pl.ds(r * seg, seg)
            qr[dst, :] = q_ref[src, :].astype(BF16)
            kr[dst, :] = k_ref[src, :].astype(BF16)
            vr[dst, :] = v_ref[src, :].astype(BF16)
        for qb in range(seq // blk):
            cur = pl.ds(qb * blk, blk)
            has_prev = (qb * blk) % seg != 0
            prev = pl.ds((qb - 1) * blk, blk)
            o, lse = _banded_block(qr[cur, :], kr[cur, :], vr[cur, :],
                                   kr[prev, :] if has_prev else None,
                                   vr[prev, :] if has_prev else None, slope_d)
            ores[cur, :] = o
            lres[cur, :] = jnp.broadcast_to(lse, (blk, LANES))
        for r in range(d):
            dst = pl.ds(r, seg, stride=d) if d > 1 else pl.ds(0, seg)
            src = pl.ds(r * seg, seg)
            opos[g][dst, :] = ores[src, :]
            lpos[g][dst, :] = lres[src, :]
    chunk = 256
    for c in range(seq // chunk):
        rows = pl.ds(c * chunk, chunk)
        outs = _combine_groups([opos[g][rows, :] for g in range(ng)], [lpos[g][rows, :] for g in range(ng)])
        for g in range(ng):
            o_refs[g][rows, :] = outs[g].astype(o_refs[g].dtype)


def _dil_prompt(qkv, slopes, n_rows, batch, seq):
    assert n_rows == batch * seq
    ng = len(DIL_GROUPS)
    for w, d in DIL_GROUPS:
        assert w // d == LANES and seq % (d * LANES) == 0
    in_specs = [pl.BlockSpec(memory_space=pltpu.SMEM)]
    for g in range(ng):
        for part in range(3):
            in_specs.append(pl.BlockSpec(
                (seq, HEAD_DIM),
                lambda b, s, g=g, part=part: (b, part * N_SELF_HEADS + g * HEADS_PER_GROUP + s)))
    out_spec = pl.BlockSpec((seq, HEAD_DIM), lambda b, s: (b, s))
    scratch = [pltpu.VMEM((seq, HEAD_DIM), BF16)] * 3 + [pltpu.VMEM((seq, HEAD_DIM), F32)] * (2 + 2 * ng)
    return pl.pallas_call(
        _dil_prompt_kernel,
        out_shape=tuple(jax.ShapeDtypeStruct((n_rows, GROUP_W), BF16) for _ in range(ng)),
        grid=(batch, HEADS_PER_GROUP),
        in_specs=in_specs,
        out_specs=tuple(out_spec for _ in range(ng)),
        scratch_shapes=scratch,
        compiler_params=_params("arbitrary", "arbitrary"),
        name="dilated_prompt")(slopes, *([qkv] * (3 * ng)))


def _dil_sample_kernel(slopes_ref, *refs, n_new, past_len):
    ng = len(DIL_GROUPS)
    qkv_refs = refs[:3 * ng]
    bk_ref, bv_ref = refs[3 * ng:3 * ng + 2]
    o_refs = refs[3 * ng + 2:4 * ng + 2]
    nk_ref, nv_ref = refs[4 * ng + 2:]
    slot = pl.program_id(1)
    nq = qkv_refs[0].shape[0]
    outs, lses = [], []
    off = 0
    for g, (w, d) in enumerate(DIL_GROUPS):
        assert d & (d - 1) == 0
        q_ref, k_ref, v_ref = qkv_refs[3 * g:3 * g + 3]
        slope = slopes_ref[g * HEADS_PER_GROUP + slot]
        row = lax.broadcasted_iota(jnp.int32, (nq, w), 0)
        col = lax.broadcasted_iota(jnp.int32, (nq, w), 1)
        dist = w + row - col
        ok = (dist <= w) & ((dist & (d - 1)) == 0) & (dist <= past_len + row)
        row2 = lax.broadcasted_iota(jnp.int32, (nq, nq), 0)
        col2 = lax.broadcasted_iota(jnp.int32, (nq, nq), 1)
        dist2 = row2 - col2
        ok2 = (dist2 >= 0) & ((dist2 & (d - 1)) == 0) & (col2 < n_new)
        q = q_ref[...].astype(BF16)
        kn = k_ref[...].astype(BF16)
        vn = v_ref[...].astype(BF16)
        kb = bk_ref[off:off + w, :].astype(BF16)
        vb = bv_ref[off:off + w, :].astype(BF16)
        s1 = _dot_nt(q, kb) * ATTN_SCALE - slope * dist.astype(F32)
        s1 = jnp.where(ok, s1, NEG_INF)
        s2 = _dot_nt(q, kn) * ATTN_SCALE - slope * dist2.astype(F32)
        s2 = jnp.where(ok2, s2, NEG_INF)
        m = jnp.maximum(jnp.max(s1, axis=1, keepdims=True), jnp.max(s2, axis=1, keepdims=True))
        e1 = jnp.exp(s1 - m)
        e2 = jnp.exp(s2 - m)
        l = jnp.sum(e1, axis=1, keepdims=True) + jnp.sum(e2, axis=1, keepdims=True)
        outs.append(_dot((e1 / l).astype(BF16), vb) + _dot((e2 / l).astype(BF16), vn))
        lses.append(m + jnp.log(l))
        keep = w - n_new
        step = 256
        for src_ref, new_ref, dst_ref in ((bk_ref, k_ref, nk_ref), (bv_ref, v_ref, nv_ref)):
            for c0 in range(0, keep, step):
                cn = min(step, keep - c0)
                dst_ref[off + c0:off + c0 + cn, :] = src_ref[off + n_new + c0:off + n_new + c0 + cn, :]
            dst_ref[off + keep:off + w, :] = new_ref[0:n_new, :]
        off += w
    mixed = _combine_groups(outs, lses)
    for g in range(ng):
        o_refs[g][...] = mixed[g]


def _dil_sample(slopes, qs, ks, vs, buf_k, buf_v, layer, n_new, past_len):
    ng = len(DIL_GROUPS)
    db, nq, _ = qs.shape
    rows = buf_k.shape[2]
    in_specs = [pl.BlockSpec(memory_space=pltpu.SMEM)]
    args = [slopes]
    for g in range(ng):
        for part, arr in enumerate((qs, ks, vs)):
            in_specs.append(pl.BlockSpec((None, nq, HEAD_DIM), lambda b, s, g=g: (b, 0, g * HEADS_PER_GROUP + s)))
            args.append(arr)
    buf_spec = pl.BlockSpec((None, None, rows, HEAD_DIM), lambda b, s: (layer, b, 0, s))
    in_specs += [buf_spec, buf_spec]
    args += [buf_k, buf_v]
    o_spec = pl.BlockSpec((None, nq, HEAD_DIM), lambda b, s: (b, 0, s))
    new_spec = pl.BlockSpec((None, rows, HEAD_DIM), lambda b, s: (b, 0, s))
    res = pl.pallas_call(
        functools.partial(_dil_sample_kernel, n_new=n_new, past_len=past_len),
        out_shape=tuple([jax.ShapeDtypeStruct((db, nq, GROUP_W), F32)] * ng
                        + [jax.ShapeDtypeStruct((db, rows, GROUP_W), F32)] * 2),
        grid=(db, HEADS_PER_GROUP),
        in_specs=in_specs,
        out_specs=tuple([o_spec] * ng + [new_spec] * 2),
        compiler_params=_params("arbitrary", "arbitrary"),
        name="dilated_sample")(*args)
    return res[:ng], res[ng], res[ng + 1]


def _mem_attn_kernel(q_ref, k_ref, v_ref, o_ref):
    s = _dot_nt(q_ref[...].astype(BF16), k_ref[...].astype(BF16)) * ATTN_SCALE
    m = jnp.max(s, axis=1, keepdims=True)
    e = jnp.exp(s - m)
    p = e / jnp.sum(e, axis=1, keepdims=True)
    o_ref[...] = _dot(p.astype(BF16), v_ref[...].astype(BF16)).astype(o_ref.dtype)


def _mem_attn(q, q_col0, mk, mv, layer, batch, rows_per_batch, q_tile, n_rows, out_dtype):
    assert n_rows == batch * rows_per_batch
    nq = rows_per_batch // q_tile
    n_mem = mk.shape[2]
    kv_spec = pl.BlockSpec((None, None, n_mem, HEAD_DIM), lambda b, h, i: (layer, b, 0, h))
    return pl.pallas_call(
        _mem_attn_kernel,
        out_shape=jax.ShapeDtypeStruct((n_rows, MEM_W), out_dtype),
        grid=(batch, N_MEM_HEADS, nq),
        in_specs=[pl.BlockSpec((q_tile, HEAD_DIM), lambda b, h, i: (b * nq + i, q_col0 + h)), kv_spec, kv_spec],
        out_specs=pl.BlockSpec((q_tile, HEAD_DIM), lambda b, h, i: (b * nq + i, h)),
        compiler_params=_params("arbitrary", "arbitrary", "arbitrary"),
        name="mem_attn")(q, mk, mv)


def _oproj_kernel(x0, x1, x2, x3, w0, w1, w2, w3, y_ref, g_ref, b_ref, rwh_ref, rwl_ref, rb_ref,
                  y1_ref, y16_ref, lg_ref, *, alpha):
    h = _dot(x0[...], w0[...]) + _dot(x1[...], w1[...]) + _dot(x2[...], w2[...]) + _dot(x3[...], w3[...])
    y = _layer_norm(alpha * y_ref[...] + h, g_ref[...], b_ref[...])
    y1_ref[...] = y
    hi = y.astype(BF16)
    y16_ref[...] = hi
    lo = (y - hi.astype(F32)).astype(BF16)
    lg_ref[...] = _dot(hi, rwh_ref[...]) + (_dot(lo, rwh_ref[...]) + _dot(hi, rwl_ref[...])) + rb_ref[...]


def _oproj(xs, x_cols, w_out16, layer, y, g, b, rw_hi, rw_lo, rb, alpha):
    t, d = y.shape
    tm = 256
    x_specs = [pl.BlockSpec((tm, GROUP_W), lambda i, c=c: (i, c)) for c in x_cols]
    w_specs = [pl.BlockSpec((None, None, GROUP_W, d), lambda i, c=c: (layer, c, 0, 0)) for c in range(4)]
    row = pl.BlockSpec((tm, d), lambda i: (i, 0))
    vec = pl.BlockSpec((1, d), lambda i: (0, 0))
    rw = pl.BlockSpec((d, LANES), lambda i: (0, 0))
    lg = pl.BlockSpec((tm, LANES), lambda i: (i, 0))
    return pl.pallas_call(
        functools.partial(_oproj_kernel, alpha=alpha),
        out_shape=(jax.ShapeDtypeStruct((t, d), F32), jax.ShapeDtypeStruct((t, d), BF16),
                   jax.ShapeDtypeStruct((t, LANES), F32)),
        grid=(t // tm,),
        in_specs=x_specs + w_specs + [row, vec, vec, rw, rw, pl.BlockSpec((1, LANES), lambda i: (0, 0))],
        out_specs=(row, row, lg),
        compiler_params=_params("arbitrary"),
        name="out_proj_norm_router")(*xs, w_out16, w_out16, w_out16, w_out16, y, g, b, rw_hi, rw_lo, rb)


def _row_copy(src_ref, src_row, dst_ref, dst_row, sem):
    return pltpu.make_async_copy(src_ref.at[pl.ds(src_row, 1), :], dst_ref.at[pl.ds(dst_row, 1), :], sem.at[0])


def _dispatch_kernel(pos_ref, y_ref, xs_in_ref, xs_ref, packed_ref, sem, *, n_tok):
    del xs_in_ref
    tm, d = y_ref.shape
    half = d // 2
    lo = pltpu.bitcast(y_ref[:, :half].astype(F32), jnp.uint32) >> 16
    hi = pltpu.bitcast(y_ref[:, half:].astype(F32), jnp.uint32) & jnp.uint32(0xFFFF0000)
    packed_ref[...] = hi | lo
    n_valid = jnp.clip(n_tok - pl.program_id(0) * tm, 0, tm)

    def start(r, carry):
        for k in range(TOP_K):
            _row_copy(packed_ref, r, xs_ref, pos_ref[0, r * TOP_K + k], sem).start()
        return carry

    def wait(r, carry):
        for k in range(TOP_K):
            _row_copy(packed_ref, 0, xs_ref, 0, sem).wait()
        return carry

    lax.fori_loop(0, n_valid, start, 0)
    lax.fori_loop(0, n_valid, wait, 0)


def _dispatch(pos_tiles, y16, n_rows, n_tok):
    t, d = y16.shape
    tm = TOKEN_TILE
    n_steps = -(-n_tok // tm)
    zeros = jnp.zeros((n_rows, d // 2), jnp.uint32)
    return pl.pallas_call(
        functools.partial(_dispatch_kernel, n_tok=n_tok),
        out_shape=jax.ShapeDtypeStruct((n_rows, d // 2), jnp.uint32),
        grid=(n_steps,),
        in_specs=[pl.BlockSpec((None, 1, tm * TOP_K), lambda i: (i, 0, 0), memory_space=pltpu.SMEM),
                  pl.BlockSpec((tm, d), lambda i: (i, 0)),
                  pl.BlockSpec(memory_space=pl.ANY)],
        out_specs=pl.BlockSpec(memory_space=pl.ANY),
        scratch_shapes=[pltpu.VMEM((tm, d // 2), jnp.uint32), pltpu.SemaphoreType.DMA((1,))],
        input_output_aliases={2: 0},
        compiler_params=_params("arbitrary"),
        name="moe_dispatch")(pos_tiles, y16, zeros)


def _combine_kernel(pos_ref, gates_ref, y_ref, g_ref, b_ref, src_ref, o_ref, o16_ref, buf_ref, sem, *, alpha):
    tm = y_ref.shape[0]

    def copy(r, k):
        return pltpu.make_async_copy(src_ref.at[pl.ds(pos_ref[0, r * TOP_K + k], 1), :],
                                     buf_ref.at[k, pl.ds(r, 1), :], sem.at[0])

    def start(r, carry):
        for k in range(TOP_K):
            copy(r, k).start()
        return carry

    def wait(r, carry):
        for k in range(TOP_K):
            copy(0, k).wait()
        return carry

    lax.fori_loop(0, tm, start, 0)
    lax.fori_loop(0, tm, wait, 0)
    gates = gates_ref[...]
    f = gates[:, 0:1] * buf_ref[0]
    for k in range(1, TOP_K):
        f = f + gates[:, k:k + 1] * buf_ref[k]
    y = _layer_norm(alpha * y_ref[...] + f, g_ref[...], b_ref[...])
    o_ref[...] = y
    o16_ref[...] = y.astype(BF16)


def _combine(pos_tiles, gates, y, g, b, out_sorted, alpha):
    t, d = y.shape
    tm = TOKEN_TILE
    row = pl.BlockSpec((tm, d), lambda i: (i, 0))
    vec = pl.BlockSpec((1, d), lambda i: (0, 0))
    return pl.pallas_call(
        functools.partial(_combine_kernel, alpha=alpha),
        out_shape=(jax.ShapeDtypeStruct((t, d), F32), jax.ShapeDtypeStruct((t, d), BF16)),
        grid=(t // tm,),
        in_specs=[pl.BlockSpec((None, 1, tm * TOP_K), lambda i: (i, 0, 0), memory_space=pltpu.SMEM),
                  pl.BlockSpec((tm, TOP_K), lambda i: (i, 0)), row, vec, vec,
                  pl.BlockSpec(memory_space=pl.ANY)],
        out_specs=(row, row),
        scratch_shapes=[pltpu.VMEM((TOP_K, tm, d), F32), pltpu.SemaphoreType.DMA((1,))],
        compiler_params=_params("arbitrary"),
        name="moe_combine_norm")(pos_tiles, gates, y, g, b, out_sorted)

def _moe_up_kernel(te_ref, tv_ref, nw_ref, x_ref, wg_ref, wl_ref, bg_ref, bl_ref, o_ref, wgb_ref, wlb_ref):
    t = pl.program_id(1)

    @pl.when(nw_ref[t] == 1)
    def _():
        wgb_ref[...] = wg_ref[...].astype(BF16)
        wlb_ref[...] = wl_ref[...].astype(BF16)

    @pl.when(tv_ref[t] == 1)
    def _():
        xu = x_ref[...]
        half = xu.shape[1]
        x_lo = pltpu.bitcast(xu << 16, F32).astype(BF16)
        x_hi = pltpu.bitcast(xu & jnp.uint32(0xFFFF0000), F32).astype(BF16)
        hg = _dot(x_lo, wgb_ref[:half, :]) + _dot(x_hi, wgb_ref[half:, :]) + bg_ref[...]
        hl = _dot(x_lo, wlb_ref[:half, :]) + _dot(x_hi, wlb_ref[half:, :]) + bl_ref[...]
        gate = jnp.minimum(hg, SWIGLU_LIMIT)
        lin = jnp.clip(hl, -SWIGLU_LIMIT, SWIGLU_LIMIT)
        act = gate * (1.0 / (1.0 + jnp.exp(-SWIGLU_ALPHA * gate))) * (lin + 1.0)
        o_ref[...] = act.astype(o_ref.dtype)

    @pl.when(tv_ref[t] == 0)
    def _():
        o_ref[...] = jnp.zeros(o_ref.shape, o_ref.dtype)


def _moe_up(tile_e, tile_valid, new_w, xs, w1, b1, layer):
    n_tiles = tile_e.shape[0]
    m = n_tiles * MOE_TILE
    d = w1.shape[2]
    f = w1.shape[3] // 2
    tn = MOE_UP_COLS
    nj = f // tn
    grid_spec = pltpu.PrefetchScalarGridSpec(
        num_scalar_prefetch=3, grid=(nj, n_tiles),
        in_specs=[
            pl.BlockSpec((MOE_TILE, d // 2), lambda j, t, te, tv, nw: (t, 0)),
            pl.BlockSpec((None, None, d, tn), lambda j, t, te, tv, nw: (layer, te[t], 0, j)),
            pl.BlockSpec((None, None, d, tn), lambda j, t, te, tv, nw: (layer, te[t], 0, nj + j)),
            pl.BlockSpec((None, None, 1, tn), lambda j, t, te, tv, nw: (layer, te[t], 0, j)),
            pl.BlockSpec((None, None, 1, tn), lambda j, t, te, tv, nw: (layer, te[t], 0, nj + j)),
        ],
        out_specs=pl.BlockSpec((MOE_TILE, tn), lambda j, t, te, tv, nw: (t, j)),
        scratch_shapes=[pltpu.VMEM((d, tn), BF16), pltpu.VMEM((d, tn), BF16)])
    return pl.pallas_call(
        _moe_up_kernel,
        out_shape=jax.ShapeDtypeStruct((m, f), BF16),
        grid_spec=grid_spec,
        compiler_params=_params("arbitrary", "arbitrary"),
        name="moe_up")(tile_e, tile_valid, new_w, xs, w1, w1, b1, b1)


def _moe_down_kernel(te_ref, tv_ref, nw_ref, a_ref, w_ref, b_ref, o_ref, wb_ref):
    t = pl.program_id(1)

    @pl.when(nw_ref[t] == 1)
    def _():
        wb_ref[...] = w_ref[...].astype(BF16)

    @pl.when(tv_ref[t] == 1)
    def _():
        o_ref[...] = _dot(a_ref[...], wb_ref[...]) + b_ref[...]

    @pl.when(tv_ref[t] == 0)
    def _():
        o_ref[...] = jnp.zeros(o_ref.shape, o_ref.dtype)


def _moe_down(tile_e, tile_valid, new_w, act, w2, b2, layer):
    m, f = act.shape
    d = w2.shape[3]
    tn = MOE_DOWN_COLS
    grid_spec = pltpu.PrefetchScalarGridSpec(
        num_scalar_prefetch=3, grid=(d // tn, m // MOE_TILE),
        in_specs=[
            pl.BlockSpec((MOE_TILE, f), lambda j, t, te, tv, nw: (t, 0)),
            pl.BlockSpec((None, None, f, tn), lambda j, t, te, tv, nw: (layer, te[t], 0, j)),
            pl.BlockSpec((None, None, 1, tn), lambda j, t, te, tv, nw: (layer, te[t], 0, j)),
        ],
        out_specs=pl.BlockSpec((MOE_TILE, tn), lambda j, t, te, tv, nw: (t, j)),
        scratch_shapes=[pltpu.VMEM((f, tn), BF16)])
    return pl.pallas_call(
        _moe_down_kernel,
        out_shape=jax.ShapeDtypeStruct((m, d), F32),
        grid_spec=grid_spec,
        compiler_params=_params("arbitrary", "arbitrary"),
        name="moe_down")(tile_e, tile_valid, new_w, act, w2, b2)


def _route(logits):
    t = logits.shape[0]
    top_val, top_idx = lax.top_k(logits, TOP_K)
    gates = jax.nn.softmax(top_val, axis=-1)
    top_idx = top_idx.astype(jnp.int32)
    experts = jnp.arange(N_EXPERTS, dtype=jnp.int32)
    onehot = jnp.sum((top_idx[:, :, None] == experts[None, None, :]).astype(jnp.int32), axis=1)
    csum = jnp.cumsum(onehot, axis=0)
    counts = csum[-1]
    rank = jnp.take_along_axis(csum - onehot, top_idx, axis=1)
    padded = (counts + MOE_TILE - 1) // MOE_TILE * MOE_TILE
    pend = jnp.cumsum(padded)
    pstart = pend - padded
    pos = pstart[top_idx] + rank
    n_tiles = -(-(t * TOP_K + N_EXPERTS * (MOE_TILE - 1)) // MOE_TILE)
    tile_start = jnp.arange(n_tiles, dtype=jnp.int32) * MOE_TILE
    valid = tile_start < pend[-1]
    raw = jnp.minimum(jnp.sum((tile_start[:, None] >= pend[None, :]).astype(jnp.int32), axis=1), N_EXPERTS - 1)
    tile_e = jnp.where(valid, raw, jnp.max(jnp.where(valid, raw, 0)))
    new_w = jnp.concatenate([jnp.ones((1,), jnp.int32), (tile_e[1:] != tile_e[:-1]).astype(jnp.int32)])
    return gates, pos, tile_e, valid.astype(jnp.int32), new_w


def _alibi_slopes(n):
    return 2.0 ** (-8.0 * jnp.arange(1, n + 1, dtype=F32) / n)


def kernel(x_prompt, x_sample, cache_fox_k, cache_fox_v, cache_fox_logf, cache_win_k, cache_win_v,
           cache_mem_k, cache_mem_v, page_table, mem_prompt, w_in_a, b_fgate, w_in_b, w_mem_kv, w_out,
           ln1_g, ln1_b, router_w, router_b, w1, b1, w2, b2, ln2_g, ln2_b):
    B, S, D = x_prompt.shape
    DB, n, _ = x_sample.shape
    depth = w_out.shape[0]
    n_mem = mem_prompt.shape[1]
    n_prompt = B * S
    n_tok = n_prompt + DB * n
    t_pad = -(-n_tok // ROW_TILE) * ROW_TILE
    s_blk = n_prompt // ROW_TILE
    assert n_prompt % ROW_TILE == 0 and DB * n <= ROW_TILE and n <= SUBLANES
    alpha = float((2 * depth) ** 0.25)
    slopes = _alibi_slopes(N_SELF_HEADS)
    n_pool, page = cache_fox_k.shape[1], cache_fox_k.shape[2]

    tok = jnp.concatenate([x_prompt.reshape(n_prompt, D), x_sample.reshape(DB * n, D),
                           jnp.zeros((t_pad - n_tok, D), F32)], axis=0)
    tok16 = tok.astype(BF16)

    mem_k_p, mem_v_p = _mem_kv(mem_prompt.reshape(B * n_mem, D), w_mem_kv)
    mem_k_p = mem_k_p.reshape(depth, B, n_mem, MEM_W)
    mem_v_p = mem_v_p.reshape(depth, B, n_mem, MEM_W)
    cmk = cache_mem_k.reshape(depth, DB, n_mem, MEM_W)
    cmv = cache_mem_v.reshape(depth, DB, n_mem, MEM_W)

    kc = cache_fox_k.transpose(0, 1, 3, 2, 4)
    vc = cache_fox_v.transpose(0, 1, 3, 2, 4)
    lfc = cache_fox_logf.transpose(0, 1, 3, 2)
    win_rows = cache_win_k.shape[2]
    wk = cache_win_k.reshape(cache_win_k.shape[0], DB, win_rows, GROUP_W)
    wv = cache_win_v.reshape(cache_win_v.shape[0], DB, win_rows, GROUP_W)

    w_out16 = w_out.astype(BF16).reshape(depth, 4, GROUP_W, D)
    b1r = b1.reshape(b1.shape[0], b1.shape[1], 1, b1.shape[2])
    b2r = b2.reshape(b2.shape[0], b2.shape[1], 1, b2.shape[2])

    def sample_rows(arr, c0, width):
        r = arr[n_prompt:n_tok, c0:c0 + width].reshape(DB, n, width)
        return jnp.pad(r, ((0, 0), (0, SUBLANES - n), (0, 0)))

    def put_sample(table, rows):
        width = table.shape[1]
        blk = jnp.zeros((t_pad - n_prompt, width), table.dtype).at[:DB * n].set(
            rows.reshape(DB * n, width).astype(table.dtype))
        return jnp.concatenate([table, blk], axis=0)

    fk_p, fv_p, flf_p, fk_s, fv_s, flf_s = [], [], [], [], [], []
    wk_p, wv_p, wk_s, wv_s = [], [], [], []
    for i in range(depth):
        j = i // N_MIXERS
        if i % N_MIXERS == 0:
            qkv = _proj(tok16, w_in_a, j, 3 * SELF_W)
            w_f = w_in_a[j][:, 3 * SELF_W:3 * SELF_W + N_SELF_HEADS]
            w_qm = w_in_a[j][:, 3 * SELF_W + N_SELF_HEADS:]
            qm = _proj(tok16, w_qm[None], 0, MEM_W)
            qm_col0 = 0
            hp = 2 * SUBLANES
            wf_t = jnp.pad(w_f.T, ((0, hp - N_SELF_HEADS), (0, 0))).astype(BF16)
            b_col = jnp.pad(b_fgate[j], (0, hp - N_SELF_HEADS)).reshape(hp, 1)
            lf_p, c_p = _gate(tok16, wf_t, b_col, S, 0, B)
            lf_s, _ = _gate(tok16, wf_t, b_col, ROW_TILE, s_blk, 1)
            ck = c_p[:, :N_SELF_HEADS].reshape(B, N_SELF_HEADS, 1, S)
            o_self = _fox_prompt(qkv, ck, n_prompt, B, S)
            q_s = sample_rows(qkv, 0, SELF_W)
            k_s = sample_rows(qkv, SELF_W, SELF_W)
            v_s = sample_rows(qkv, 2 * SELF_W, SELF_W)
            lf_s = lf_s[0, :N_SELF_HEADS, :DB * n].reshape(N_SELF_HEADS, DB, n).transpose(1, 0, 2)
            def head_major(x, rows):
                xh = x.reshape(DB, SUBLANES, N_SELF_HEADS, HEAD_DIM).transpose(0, 2, 1, 3)
                return jnp.pad(xh, ((0, 0), (0, 0), (0, rows - SUBLANES), (0, 0)))

            q16 = head_major(q_s, HEAD_ROWS).reshape(DB, N_SELF_HEADS * HEAD_ROWS, HEAD_DIM).astype(BF16)
            lfnew = jnp.pad(lf_s, ((0, 0), (0, 0), (0, page - n)))
            o_s = _fox_decode(page_table, q16, kc, vc, lfc, j, head_major(k_s, page), head_major(v_s, page),
                              lfnew, n)
            o_s = o_s.reshape(DB, N_SELF_HEADS, HEAD_ROWS, HEAD_DIM)[:, :, :n].transpose(0, 2, 1, 3)
            o_self = put_sample(o_self, o_s.reshape(DB, n, SELF_W))
            xs_parts, x_cols = [o_self, o_self, o_self], [0, 1, 2]
            fk_p.append(qkv[:n_prompt, SELF_W:2 * SELF_W].reshape(B, S, N_SELF_HEADS, HEAD_DIM))
            fv_p.append(qkv[:n_prompt, 2 * SELF_W:3 * SELF_W].reshape(B, S, N_SELF_HEADS, HEAD_DIM))
            flf_p.append(lf_p[:, :N_SELF_HEADS].transpose(0, 2, 1))
            fk_s.append(k_s[:, :n].reshape(DB, n, N_SELF_HEADS, HEAD_DIM))
            fv_s.append(v_s[:, :n].reshape(DB, n, N_SELF_HEADS, HEAD_DIM))
            flf_s.append(lf_s.transpose(0, 2, 1))
        else:
            qkv = _proj(tok16, w_in_b, j, 3 * SELF_W + MEM_W)
            qm, qm_col0 = qkv, 3 * N_SELF_HEADS
            o_groups = _dil_prompt(qkv, slopes, n_prompt, B, S)
            q_s = sample_rows(qkv, 0, SELF_W)
            k_s = sample_rows(qkv, SELF_W, SELF_W)
            v_s = sample_rows(qkv, 2 * SELF_W, SELF_W)
            o_s, nk, nv = _dil_sample(slopes, q_s, k_s, v_s, wk, wv, j, n, page_table.shape[1] * page)
            xs_parts = [put_sample(o_groups[g], o_s[g][:, :n]) for g in range(len(DIL_GROUPS))]
            x_cols = [0, 0, 0]
            kp = qkv[:n_prompt, SELF_W:2 * SELF_W].reshape(B, S, N_SELF_HEADS, HEAD_DIM)
            vp = qkv[:n_prompt, 2 * SELF_W:3 * SELF_W].reshape(B, S, N_SELF_HEADS, HEAD_DIM)
            bk, bv = [], []
            for g, (w, d) in enumerate(DIL_GROUPS):
                hs = slice(g * HEADS_PER_GROUP, (g + 1) * HEADS_PER_GROUP)
                pad = ((0, 0), (w, 0), (0, 0), (0, 0))
                bk.append(jnp.pad(kp[:, :, hs], pad)[:, -w:])
                bv.append(jnp.pad(vp[:, :, hs], pad)[:, -w:])
            wk_p.append(jnp.concatenate(bk, axis=1))
            wv_p.append(jnp.concatenate(bv, axis=1))
            wk_s.append(nk.reshape(DB, win_rows, HEADS_PER_GROUP, HEAD_DIM))
            wv_s.append(nv.reshape(DB, win_rows, HEADS_PER_GROUP, HEAD_DIM))
        o_mem = _mem_attn(qm, qm_col0, mem_k_p, mem_v_p, i, B, S, ATTN_TILE, n_prompt, BF16)
        qm_s = sample_rows(qm, qm_col0 * HEAD_DIM, MEM_W).reshape(DB * SUBLANES, MEM_W)
        om_s = _mem_attn(qm_s, 0, cmk, cmv, i, DB, SUBLANES, SUBLANES, DB * SUBLANES, F32)
        o_mem = put_sample(o_mem, om_s.reshape(DB, SUBLANES, MEM_W)[:, :n])
        rw = jnp.pad(router_w[i], ((0, 0), (0, LANES - N_EXPERTS)))
        rw_hi = rw.astype(BF16)
        rw_lo = (rw - rw_hi.astype(F32)).astype(BF16)
        rb = jnp.pad(router_b[i], (0, LANES - N_EXPERTS)).reshape(1, LANES)
        y1, y16, logits = _oproj(xs_parts + [o_mem], x_cols + [0], w_out16, i, tok,
                                 ln1_g[i].reshape(1, D), ln1_b[i].reshape(1, D), rw_hi, rw_lo, rb, alpha)
        gates, pos, tile_e, tile_valid, new_w = _route(logits[:n_tok, :N_EXPERTS])
        gates = jnp.pad(gates, ((0, t_pad - n_tok), (0, 0)))
        pos_tiles = jnp.pad(pos, ((0, t_pad - n_tok), (0, 0))).reshape(
            t_pad // TOKEN_TILE, 1, TOKEN_TILE * TOP_K)
        xs = _dispatch(pos_tiles, y16, tile_e.shape[0] * MOE_TILE, n_tok)
        act = _moe_up(tile_e, tile_valid, new_w, xs, w1, b1r, i)
        out_sorted = _moe_down(tile_e, tile_valid, new_w, act, w2, b2r, i)
        tok, tok16 = _combine(pos_tiles, gates, y1, ln2_g[i].reshape(1, D), ln2_b[i].reshape(1, D),
                              out_sorted, alpha)

    yp = tok[:n_prompt].reshape(B, S, D)
    ys = tok[n_prompt:n_tok].reshape(DB, n, D)
    mk_out = mem_k_p.reshape(depth, B, n_mem, N_MEM_HEADS, HEAD_DIM)
    mv_out = mem_v_p.reshape(depth, B, n_mem, N_MEM_HEADS, HEAD_DIM)
    return (yp, ys, jnp.stack(fk_p), jnp.stack(fv_p), jnp.stack(flf_p), jnp.stack(wk_p), jnp.stack(wv_p),
            mk_out, mv_out, jnp.stack(fk_s), jnp.stack(fv_s), jnp.stack(flf_s), jnp.stack(wk_s), jnp.stack(wv_s))
```

```python
import functools

import numpy as np
import jax
import jax.numpy as jnp
from jax import lax
from jax.experimental import pallas as pl
from jax.experimental.pallas import tpu as pltpu

HEAD_DIM = 128
N_SELF_HEADS = 12
N_MEM_HEADS = 4
DIL_GROUPS = ((128, 1), (512, 4), (2048, 16))
HEADS_PER_GROUP = N_SELF_HEADS // len(DIL_GROUPS)
N_EXPERTS = 32
TOP_K = 4
SWIGLU_ALPHA = 1.702
SWIGLU_LIMIT = 7.0
N_MIXERS = 2
LN_EPS = 1e-5
ATTN_SCALE = HEAD_DIM ** -0.5
SELF_W = N_SELF_HEADS * HEAD_DIM
MEM_W = N_MEM_HEADS * HEAD_DIM
GROUP_W = HEADS_PER_GROUP * HEAD_DIM

LANES = 128
SUBLANES = 8
ROW_TILE = 512
COL_TILE = 512
ATTN_TILE = 512
MOE_TILE = 256
MOE_UP_COLS = 1024
MOE_DOWN_COLS = 2048
TOKEN_TILE = 128
PAGES_PER_STEP = 8
HEAD_ROWS = 16
VMEM_LIMIT = 56 * 1024 * 1024

BF16 = jnp.bfloat16
F32 = jnp.float32
NEG_INF = float("-inf")


def _params(*sem):
    return pltpu.CompilerParams(dimension_semantics=sem, vmem_limit_bytes=VMEM_LIMIT)


def _dot(a, b):
    return jnp.dot(a, b, preferred_element_type=F32)


def _dot_nt(a, b):
    return lax.dot_general(a, b, (((1,), (1,)), ((), ())), preferred_element_type=F32)


def _scan_lanes(v):
    lane = lax.broadcasted_iota(jnp.int32, v.shape, 1)
    shift = 1
    while shift < LANES:
        v = v + jnp.where(lane >= shift, pltpu.roll(v, shift, axis=1), 0.0)
        shift *= 2
    return v


def _layer_norm(r, g, b):
    mu = jnp.mean(r, axis=-1, keepdims=True)
    xc = r - mu
    var = jnp.mean(xc * xc, axis=-1, keepdims=True)
    return xc * lax.rsqrt(var + LN_EPS) * g + b


def _proj_kernel(x_ref, w_ref, o_ref, wbf_ref):
    @pl.when(pl.program_id(1) == 0)
    def _():
        wbf_ref[...] = w_ref[...].astype(BF16)

    o_ref[...] = _dot(x_ref[...], wbf_ref[...])


def _proj(x16, w, layer, n_cols):
    t, k = x16.shape
    return pl.pallas_call(
        _proj_kernel,
        out_shape=jax.ShapeDtypeStruct((t, n_cols), F32),
        grid=(n_cols // COL_TILE, t // ROW_TILE),
        in_specs=[pl.BlockSpec((ROW_TILE, k), lambda j, i: (i, 0)),
                  pl.BlockSpec((None, k, COL_TILE), lambda j, i: (layer, 0, j))],
        out_specs=pl.BlockSpec((ROW_TILE, COL_TILE), lambda j, i: (i, j)),
        scratch_shapes=[pltpu.VMEM((k, COL_TILE), BF16)],
        compiler_params=_params("arbitrary", "arbitrary"),
        name="proj")(x16, w)


def _memkv_kernel(x_ref, w_ref, o_ref):
    o_ref[...] = _dot(x_ref[...].astype(BF16), w_ref[...].astype(BF16))


def _mem_kv(mem2d, w_mem_kv):
    r, d = mem2d.shape
    depth = w_mem_kv.shape[0]
    tm = min(ROW_TILE, r)
    assert r % tm == 0
    out = pl.pallas_call(
        _memkv_kernel,
        out_shape=jax.ShapeDtypeStruct((depth, 2, r, MEM_W), F32),
        grid=(depth, 2, r // tm),
        in_specs=[pl.BlockSpec((tm, d), lambda l, c, i: (i, 0)),
                  pl.BlockSpec((None, d, MEM_W), lambda l, c, i: (l, 0, c))],
        out_specs=pl.BlockSpec((None, None, tm, MEM_W), lambda l, c, i: (l, c, i, 0)),
        compiler_params=_params("arbitrary", "arbitrary", "arbitrary"),
        name="mem_kv")(mem2d, w_mem_kv)
    return out[:, 0], out[:, 1]


def _gate_kernel(x_ref, wf_ref, bf_ref, lf_ref, c_ref):
    z = _dot_nt(wf_ref[...], x_ref[...]) + bf_ref[...]
    lf = jnp.minimum(z, 0.0) - jnp.log1p(jnp.exp(-jnp.abs(z)))
    lf_ref[...] = lf
    carry = jnp.zeros((lf.shape[0], 1), F32)
    for blk in range(lf.shape[1] // LANES):
        c = _scan_lanes(lf[:, blk * LANES:(blk + 1) * LANES]) + carry
        c_ref[:, blk * LANES:(blk + 1) * LANES] = c
        carry = c[:, LANES - 1:LANES]


def _gate(x16, wf_t, b_col, blk_rows, first_block, n_blocks):
    k = x16.shape[1]
    hp = wf_t.shape[0]
    shp = jax.ShapeDtypeStruct((n_blocks, hp, blk_rows), F32)
    return pl.pallas_call(
        _gate_kernel,
        out_shape=(shp, shp),
        grid=(n_blocks,),
        in_specs=[pl.BlockSpec((blk_rows, k), lambda b: (first_block + b, 0)),
                  pl.BlockSpec((hp, k), lambda b: (0, 0)),
                  pl.BlockSpec((hp, 1), lambda b: (0, 0))],
        out_specs=(pl.BlockSpec((None, hp, blk_rows), lambda b: (b, 0, 0)),
                   pl.BlockSpec((None, hp, blk_rows), lambda b: (b, 0, 0))),
        compiler_params=_params("arbitrary"),
        name="forget_gate")(x16, wf_t, b_col)


def _fox_prompt_kernel(q_ref, k_ref, v_ref, ck_ref, o_ref, m_ref, l_ref, acc_ref):
    qi = pl.program_id(2)
    ki = pl.program_id(3)

    @pl.when(ki == 0)
    def _():
        m_ref[...] = jnp.full(m_ref.shape, NEG_INF, F32)
        l_ref[...] = jnp.zeros(l_ref.shape, F32)
        acc_ref[...] = jnp.zeros(acc_ref.shape, F32)

    def update(diagonal):
        s = _dot_nt(q_ref[...].astype(BF16), k_ref[...].astype(BF16)) * ATTN_SCALE
        s = s - ck_ref[...]
        if diagonal:
            row = lax.broadcasted_iota(jnp.int32, s.shape, 0)
            col = lax.broadcasted_iota(jnp.int32, s.shape, 1)
            s = jnp.where(col <= row, s, NEG_INF)
        m_prev = m_ref[...]
        m_new = jnp.maximum(m_prev, jnp.max(s, axis=1, keepdims=True))
        alpha = jnp.exp(m_prev - m_new)
        p = jnp.exp(s - m_new)
        l_ref[...] = alpha * l_ref[...] + jnp.sum(p, axis=1, keepdims=True)
        acc_ref[...] = alpha * acc_ref[...] + _dot(p.astype(BF16), v_ref[...].astype(BF16))
        m_ref[...] = m_new

    @pl.when(ki < qi)
    def _():
        update(False)

    @pl.when(ki == qi)
    def _():
        update(True)
        o_ref[...] = (acc_ref[...] / l_ref[...]).astype(o_ref.dtype)


def _fox_prompt(qkv, ck, n_rows, batch, seq):
    assert n_rows == batch * seq
    nq = seq // ATTN_TILE
    return pl.pallas_call(
        _fox_prompt_kernel,
        out_shape=jax.ShapeDtypeStruct((n_rows, SELF_W), BF16),
        grid=(batch, N_SELF_HEADS, nq, nq),
        in_specs=[
            pl.BlockSpec((ATTN_TILE, HEAD_DIM), lambda b, h, qi, ki: (b * nq + qi, h)),
            pl.BlockSpec((ATTN_TILE, HEAD_DIM),
                         lambda b, h, qi, ki: (b * nq + jnp.minimum(ki, qi), N_SELF_HEADS + h)),
            pl.BlockSpec((ATTN_TILE, HEAD_DIM),
                         lambda b, h, qi, ki: (b * nq + jnp.minimum(ki, qi), 2 * N_SELF_HEADS + h)),
            pl.BlockSpec((None, None, 1, ATTN_TILE),
                         lambda b, h, qi, ki: (b, h, 0, jnp.minimum(ki, qi))),
        ],
        out_specs=pl.BlockSpec((ATTN_TILE, HEAD_DIM), lambda b, h, qi, ki: (b * nq + qi, h)),
        scratch_shapes=[pltpu.VMEM((ATTN_TILE, 1), F32), pltpu.VMEM((ATTN_TILE, 1), F32),
                        pltpu.VMEM((ATTN_TILE, HEAD_DIM), F32)],
        compiler_params=_params("arbitrary", "arbitrary", "arbitrary", "arbitrary"),
        name="fox_prompt")(qkv, qkv, qkv, ck)


def _fox_decode_kernel(pt_ref, q_ref, *refs, n_new):
    pp = PAGES_PER_STEP
    k_refs, v_refs, lf_refs = refs[:pp], refs[pp:2 * pp], refs[2 * pp:3 * pp]
    knew_ref, vnew_ref, lfnew_ref, o_ref, m_ref, l_ref, acc_ref, carry_ref = refs[3 * pp:]
    step = pl.program_id(1)
    rows = q_ref.shape[0]
    hr = HEAD_ROWS

    @pl.when(step == 0)
    def _():
        m_ref[...] = jnp.full(m_ref.shape, NEG_INF, F32)
        l_ref[...] = jnp.zeros(l_ref.shape, F32)
        acc_ref[...] = jnp.zeros(acc_ref.shape, F32)
        carry_ref[...] = jnp.zeros(carry_ref.shape, F32)

    def block(k_ref, v_ref, lf, mask):
        lf_rep = jnp.concatenate(
            [jnp.broadcast_to(lf[h:h + 1, :], (hr, LANES)) for h in range(N_SELF_HEADS)], axis=0)
        c = _scan_lanes(lf_rep) + carry_ref[...]
        carry_ref[...] = jnp.broadcast_to(c[:, LANES - 1:LANES], c.shape)
        s = jnp.concatenate(
            [_dot_nt(q_ref[h * hr:(h + 1) * hr, :], k_ref[h].astype(BF16)) for h in range(N_SELF_HEADS)], axis=0)
        s = s * ATTN_SCALE - c
        if mask is not None:
            s = jnp.where(mask, s, NEG_INF)
        m_prev = m_ref[...]
        m_new = jnp.maximum(m_prev, jnp.max(s, axis=1, keepdims=True))
        alpha = jnp.exp(m_prev - m_new)
        p = jnp.exp(s - m_new)
        l_ref[...] = alpha * l_ref[...] + jnp.sum(p, axis=1, keepdims=True)
        p16 = p.astype(BF16)
        pv = jnp.concatenate(
            [_dot(p16[h * hr:(h + 1) * hr, :], v_ref[h].astype(BF16)) for h in range(N_SELF_HEADS)], axis=0)
        acc_ref[...] = alpha * acc_ref[...] + pv
        m_ref[...] = m_new

    for j in range(pp):
        block(k_refs[j], v_refs[j], lf_refs[j][...], None)

    @pl.when(step == pl.num_programs(1) - 1)
    def _():
        i = lax.broadcasted_iota(jnp.int32, (rows, LANES), 0) & (hr - 1)
        col = lax.broadcasted_iota(jnp.int32, (rows, LANES), 1)
        block(knew_ref, vnew_ref, lfnew_ref[...], (col <= i) & (col < n_new))
        o_ref[...] = acc_ref[...] / l_ref[...]


def _fox_decode(page_table, q16, kc, vc, lfc, layer, knew, vnew, lfnew, n_new):
    db, n_pages = page_table.shape
    pp = PAGES_PER_STEP
    assert n_pages % pp == 0 and HEAD_ROWS & (HEAD_ROWS - 1) == 0 and n_new <= HEAD_ROWS
    page = kc.shape[3]
    rows = q16.shape[1]

    def kv_spec(j):
        return pl.BlockSpec((None, None, N_SELF_HEADS, page, HEAD_DIM),
                            lambda b, s, pt: (layer, pt[b, s * pp + j], 0, 0, 0))

    def lf_spec(j):
        return pl.BlockSpec((None, None, N_SELF_HEADS, page), lambda b, s, pt: (layer, pt[b, s * pp + j], 0, 0))

    per_b3 = lambda b, s, pt: (b, 0, 0)
    per_b4 = lambda b, s, pt: (b, 0, 0, 0)
    in_specs = [pl.BlockSpec((None, rows, HEAD_DIM), per_b3)]
    in_specs += [kv_spec(j) for j in range(pp)]
    in_specs += [kv_spec(j) for j in range(pp)]
    in_specs += [lf_spec(j) for j in range(pp)]
    in_specs += [pl.BlockSpec((None, N_SELF_HEADS, page, HEAD_DIM), per_b4),
                 pl.BlockSpec((None, N_SELF_HEADS, page, HEAD_DIM), per_b4),
                 pl.BlockSpec((None, N_SELF_HEADS, page), per_b3)]
    grid_spec = pltpu.PrefetchScalarGridSpec(
        num_scalar_prefetch=1, grid=(db, n_pages // pp), in_specs=in_specs,
        out_specs=pl.BlockSpec((None, rows, HEAD_DIM), per_b3),
        scratch_shapes=[pltpu.VMEM((rows, 1), F32), pltpu.VMEM((rows, 1), F32),
                        pltpu.VMEM((rows, HEAD_DIM), F32), pltpu.VMEM((rows, LANES), F32)])
    return pl.pallas_call(
        functools.partial(_fox_decode_kernel, n_new=n_new),
        out_shape=jax.ShapeDtypeStruct((db, rows, HEAD_DIM), F32),
        grid_spec=grid_spec,
        compiler_params=_params("arbitrary", "arbitrary"),
        name="fox_decode")(page_table, q16, *([kc] * pp), *([vc] * pp), *([lfc] * pp), knew, vnew, lfnew)


def _banded_block(q, k_cur, v_cur, k_prev, v_prev, slope_d):
    n = q.shape[0]
    row = lax.broadcasted_iota(jnp.int32, (n, n), 0)
    col = lax.broadcasted_iota(jnp.int32, (n, n), 1)
    delta = (row - col).astype(F32)
    s_cur = _dot_nt(q, k_cur) * ATTN_SCALE - slope_d * delta
    s_cur = jnp.where(col <= row, s_cur, NEG_INF)
    m = jnp.max(s_cur, axis=1, keepdims=True)
    if k_prev is not None:
        s_prev = _dot_nt(q, k_prev) * ATTN_SCALE - slope_d * (delta + float(n))
        s_prev = jnp.where(col >= row, s_prev, NEG_INF)
        m = jnp.maximum(m, jnp.max(s_prev, axis=1, keepdims=True))
    e_cur = jnp.exp(s_cur - m)
    l = jnp.sum(e_cur, axis=1, keepdims=True)
    if k_prev is not None:
        e_prev = jnp.exp(s_prev - m)
        l = l + jnp.sum(e_prev, axis=1, keepdims=True)
    o = _dot((e_cur / l).astype(BF16), v_cur)
    if k_prev is not None:
        o = o + _dot((e_prev / l).astype(BF16), v_prev)
    return o, m + jnp.log(l)


def _combine_groups(outs, lses):
    mx = functools.reduce(jnp.maximum, lses)
    es = [jnp.exp(l - mx) for l in lses]
    den = functools.reduce(lambda a, b: a + b, es)
    return [o * (e / den) for o, e in zip(outs, es)]


def _dil_prompt_kernel(slopes_ref, *refs):
    ng = len(DIL_GROUPS)
    qkv_refs = refs[:3 * ng]
    o_refs = refs[3 * ng:4 * ng]
    qr, kr, vr, ores, lres = refs[4 * ng:4 * ng + 5]
    opos = refs[4 * ng + 5:5 * ng + 5]
    lpos = refs[5 * ng + 5:6 * ng + 5]
    slot = pl.program_id(1)
    seq = qr.shape[0]
    blk = LANES
    for g, (w, d) in enumerate(DIL_GROUPS):
        q_ref, k_ref, v_ref = qkv_refs[3 * g:3 * g + 3]
        seg = seq // d
        slope_d = slopes_ref[g * HEADS_PER_GROUP + slot] * float(d)
        for r in range(d):
            src = pl.ds(r, seg, stride=d) if d > 1 else pl.ds(0, seg)
            dst = pl.ds(r * seg, seg)
            qr[dst, :] = q_ref[src, :].astype(BF16)
            kr[dst, :] = k_ref[src, :].astype(BF16)
            vr[dst, :] = v_ref[src, :].astype(BF16)
        for qb in range(seq // blk):
            cur = pl.ds(qb * blk, blk)
            has_prev = (qb * blk) % seg != 0
            prev = pl.ds((qb - 1) * blk, blk)
            o, lse = _banded_block(qr[cur, :], kr[cur, :], vr[cur, :],
                                   kr[prev, :] if has_prev else None,
                                   vr[prev, :] if has_prev else None, slope_d)
            ores[cur, :] = o
            lres[cur, :] = jnp.broadcast_to(lse, (blk, LANES))
        for r in range(d):
            dst = pl.ds(r, seg, stride=d) if d > 1 else pl.ds(0, seg)
            src = pl.ds(r * seg, seg)
            opos[g][dst, :] = ores[src, :]
            lpos[g][dst, :] = lres[src, :]
    chunk = 256
    for c in range(seq // chunk):
        rows = pl.ds(c * chunk, chunk)
        outs = _combine_groups([opos[g][rows, :] for g in range(ng)], [lpos[g][rows, :] for g in range(ng)])
        for g in range(ng):
            o_refs[g][rows, :] = outs[g].astype(o_refs[g].dtype)


def _dil_prompt(qkv, slopes, n_rows, batch, seq):
    assert n_rows == batch * seq
    ng = len(DIL_GROUPS)
    for w, d in DIL_GROUPS:
        assert w // d == LANES and seq % (d * LANES) == 0
    in_specs = [pl.BlockSpec(memory_space=pltpu.SMEM)]
    for g in range(ng):
        for part in range(3):
            in_specs.append(pl.BlockSpec(
                (seq, HEAD_DIM),
                lambda b, s, g=g, part=part: (b, part * N_SELF_HEADS + g * HEADS_PER_GROUP + s)))
    out_spec = pl.BlockSpec((seq, HEAD_DIM), lambda b, s: (b, s))
    scratch = [pltpu.VMEM((seq, HEAD_DIM), BF16)] * 3 + [pltpu.VMEM((seq, HEAD_DIM), F32)] * (2 + 2 * ng)
    return pl.pallas_call(
        _dil_prompt_kernel,
        out_shape=tuple(jax.ShapeDtypeStruct((n_rows, GROUP_W), BF16) for _ in range(ng)),
        grid=(batch, HEADS_PER_GROUP),
        in_specs=in_specs,
        out_specs=tuple(out_spec for _ in range(ng)),
        scratch_shapes=scratch,
        compiler_params=_params("arbitrary", "arbitrary"),
        name="dilated_prompt")(slopes, *([qkv] * (3 * ng)))


def _dil_sample_kernel(slopes_ref, *refs, n_new, past_len):
    ng = len(DIL_GROUPS)
    qkv_refs = refs[:3 * ng]
    bk_ref, bv_ref = refs[3 * ng:3 * ng + 2]
    o_refs = refs[3 * ng + 2:4 * ng + 2]
    nk_ref, nv_ref = refs[4 * ng + 2:]
    slot = pl.program_id(1)
    nq = qkv_refs[0].shape[0]
    outs, lses = [], []
    off = 0
    for g, (w, d) in enumerate(DIL_GROUPS):
        assert d & (d - 1) == 0
        q_ref, k_ref, v_ref = qkv_refs[3 * g:3 * g + 3]
        slope = slopes_ref[g * HEADS_PER_GROUP + slot]
        row = lax.broadcasted_iota(jnp.int32, (nq, w), 0)
        col = lax.broadcasted_iota(jnp.int32, (nq, w), 1)
        dist = w + row - col
        ok = (dist <= w) & ((dist & (d - 1)) == 0) & (dist <= past_len + row)
        row2 = lax.broadcasted_iota(jnp.int32, (nq, nq), 0)
        col2 = lax.broadcasted_iota(jnp.int32, (nq, nq), 1)
        dist2 = row2 - col2
        ok2 = (dist2 >= 0) & ((dist2 & (d - 1)) == 0) & (col2 < n_new)
        q = q_ref[...].astype(BF16)
        kn = k_ref[...].astype(BF16)
        vn = v_ref[...].astype(BF16)
        kb = bk_ref[off:off + w, :].astype(BF16)
        vb = bv_ref[off:off + w, :].astype(BF16)
        s1 = _dot_nt(q, kb) * ATTN_SCALE - slope * dist.astype(F32)
        s1 = jnp.where(ok, s1, NEG_INF)
        s2 = _dot_nt(q, kn) * ATTN_SCALE - slope * dist2.astype(F32)
        s2 = jnp.where(ok2, s2, NEG_INF)
        m = jnp.maximum(jnp.max(s1, axis=1, keepdims=True), jnp.max(s2, axis=1, keepdims=True))
        e1 = jnp.exp(s1 - m)
        e2 = jnp.exp(s2 - m)
        l = jnp.sum(e1, axis=1, keepdims=True) + jnp.sum(e2, axis=1, keepdims=True)
        outs.append(_dot((e1 / l).astype(BF16), vb) + _dot((e2 / l).astype(BF16), vn))
        lses.append(m + jnp.log(l))
        keep = w - n_new
        step = 256
        for src_ref, new_ref, dst_ref in ((bk_ref, k_ref, nk_ref), (bv_ref, v_ref, nv_ref)):
            for c0 in range(0, keep, step):
                cn = min(step, keep - c0)
                dst_ref[off + c0:off + c0 + cn, :] = src_ref[off + n_new + c0:off + n_new + c0 + cn, :]
            dst_ref[off + keep:off + w, :] = new_ref[0:n_new, :]
        off += w
    mixed = _combine_groups(outs, lses)
    for g in range(ng):
        o_refs[g][...] = mixed[g]


def _dil_sample(slopes, qs, ks, vs, buf_k, buf_v, layer, n_new, past_len):
    ng = len(DIL_GROUPS)
    db, nq, _ = qs.shape
    rows = buf_k.shape[2]
    in_specs = [pl.BlockSpec(memory_space=pltpu.SMEM)]
    args = [slopes]
    for g in range(ng):
        for part, arr in enumerate((qs, ks, vs)):
            in_specs.append(pl.BlockSpec((None, nq, HEAD_DIM), lambda b, s, g=g: (b, 0, g * HEADS_PER_GROUP + s)))
            args.append(arr)
    buf_spec = pl.BlockSpec((None, None, rows, HEAD_DIM), lambda b, s: (layer, b, 0, s))
    in_specs += [buf_spec, buf_spec]
    args += [buf_k, buf_v]
    o_spec = pl.BlockSpec((None, nq, HEAD_DIM), lambda b, s: (b, 0, s))
    new_spec = pl.BlockSpec((None, rows, HEAD_DIM), lambda b, s: (b, 0, s))
    res = pl.pallas_call(
        functools.partial(_dil_sample_kernel, n_new=n_new, past_len=past_len),
        out_shape=tuple([jax.ShapeDtypeStruct((db, nq, GROUP_W), F32)] * ng
                        + [jax.ShapeDtypeStruct((db, rows, GROUP_W), F32)] * 2),
        grid=(db, HEADS_PER_GROUP),
        in_specs=in_specs,
        out_specs=tuple([o_spec] * ng + [new_spec] * 2),
        compiler_params=_params("arbitrary", "arbitrary"),
        name="dilated_sample")(*args)
    return res[:ng], res[ng], res[ng + 1]


def _mem_attn_kernel(q_ref, k_ref, v_ref, o_ref):
    s = _dot_nt(q_ref[...].astype(BF16), k_ref[...].astype(BF16)) * ATTN_SCALE
    m = jnp.max(s, axis=1, keepdims=True)
    e = jnp.exp(s - m)
    p = e / jnp.sum(e, axis=1, keepdims=True)
    o_ref[...] = _dot(p.astype(BF16), v_ref[...].astype(BF16)).astype(o_ref.dtype)


def _mem_attn(q, q_col0, mk, mv, layer, batch, rows_per_batch, q_tile, n_rows, out_dtype):
    assert n_rows == batch * rows_per_batch
    nq = rows_per_batch // q_tile
    n_mem = mk.shape[2]
    kv_spec = pl.BlockSpec((None, None, n_mem, HEAD_DIM), lambda b, h, i: (layer, b, 0, h))
    return pl.pallas_call(
        _mem_attn_kernel,
        out_shape=jax.ShapeDtypeStruct((n_rows, MEM_W), out_dtype),
        grid=(batch, N_MEM_HEADS, nq),
        in_specs=[pl.BlockSpec((q_tile, HEAD_DIM), lambda b, h, i: (b * nq + i, q_col0 + h)), kv_spec, kv_spec],
        out_specs=pl.BlockSpec((q_tile, HEAD_DIM), lambda b, h, i: (b * nq + i, h)),
        compiler_params=_params("arbitrary", "arbitrary", "arbitrary"),
        name="mem_attn")(q, mk, mv)


def _oproj_kernel(x0, x1, x2, x3, w0, w1, w2, w3, y_ref, g_ref, b_ref, rwh_ref, rwl_ref, rb_ref,
                  y1_ref, y16_ref, lg_ref, *, alpha):
    h = _dot(x0[...], w0[...]) + _dot(x1[...], w1[...]) + _dot(x2[...], w2[...]) + _dot(x3[...], w3[...])
    y = _layer_norm(alpha * y_ref[...] + h, g_ref[...], b_ref[...])
    y1_ref[...] = y
    hi = y.astype(BF16)
    y16_ref[...] = hi
    lo = (y - hi.astype(F32)).astype(BF16)
    lg_ref[...] = _dot(hi, rwh_ref[...]) + (_dot(lo, rwh_ref[...]) + _dot(hi, rwl_ref[...])) + rb_ref[...]


def _oproj(xs, x_cols, w_out16, layer, y, g, b, rw_hi, rw_lo, rb, alpha):
    t, d = y.shape
    tm = 256
    x_specs = [pl.BlockSpec((tm, GROUP_W), lambda i, c=c: (i, c)) for c in x_cols]
    w_specs = [pl.BlockSpec((None, None, GROUP_W, d), lambda i, c=c: (layer, c, 0, 0)) for c in range(4)]
    row = pl.BlockSpec((tm, d), lambda i: (i, 0))
    vec = pl.BlockSpec((1, d), lambda i: (0, 0))
    rw = pl.BlockSpec((d, LANES), lambda i: (0, 0))
    lg = pl.BlockSpec((tm, LANES), lambda i: (i, 0))
    return pl.pallas_call(
        functools.partial(_oproj_kernel, alpha=alpha),
        out_shape=(jax.ShapeDtypeStruct((t, d), F32), jax.ShapeDtypeStruct((t, d), BF16),
                   jax.ShapeDtypeStruct((t, LANES), F32)),
        grid=(t // tm,),
        in_specs=x_specs + w_specs + [row, vec, vec, rw, rw, pl.BlockSpec((1, LANES), lambda i: (0, 0))],
        out_specs=(row, row, lg),
        compiler_params=_params("arbitrary"),
        name="out_proj_norm_router")(*xs, w_out16, w_out16, w_out16, w_out16, y, g, b, rw_hi, rw_lo, rb)


def _row_copy(src_ref, src_row, dst_ref, dst_row, sem):
    return pltpu.make_async_copy(src_ref.at[pl.ds(src_row, 1), :], dst_ref.at[pl.ds(dst_row, 1), :], sem.at[0])


def _dispatch_kernel(pos_ref, y_ref, xs_in_ref, xs_ref, packed_ref, sem, *, n_tok):
    del xs_in_ref
    tm, d = y_ref.shape
    half = d // 2
    lo = pltpu.bitcast(y_ref[:, :half].astype(F32), jnp.uint32) >> 16
    hi = pltpu.bitcast(y_ref[:, half:].astype(F32), jnp.uint32) & jnp.uint32(0xFFFF0000)
    packed_ref[...] = hi | lo
    n_valid = jnp.clip(n_tok - pl.program_id(0) * tm, 0, tm)

    def start(r, carry):
        for k in range(TOP_K):
            _row_copy(packed_ref, r, xs_ref, pos_ref[0, r * TOP_K + k], sem).start(priority=k % 2)
        return carry

    def wait(r, carry):
        for k in range(TOP_K):
            _row_copy(packed_ref, 0, xs_ref, 0, sem).wait()
        return carry

    lax.fori_loop(0, n_valid, start, 0)
    lax.fori_loop(0, n_valid, wait, 0)


def _dispatch(pos_tiles, y16, n_rows, n_tok):
    t, d = y16.shape
    tm = TOKEN_TILE
    n_steps = -(-n_tok // tm)
    zeros = jnp.zeros((n_rows, d // 2), jnp.uint32)
    return pl.pallas_call(
        functools.partial(_dispatch_kernel, n_tok=n_tok),
        out_shape=jax.ShapeDtypeStruct((n_rows, d // 2), jnp.uint32),
        grid=(n_steps,),
        in_specs=[pl.BlockSpec((None, 1, tm * TOP_K), lambda i: (i, 0, 0), memory_space=pltpu.SMEM),
                  pl.BlockSpec((tm, d), lambda i: (i, 0)),
                  pl.BlockSpec(memory_space=pl.ANY)],
        out_specs=pl.BlockSpec(memory_space=pl.ANY),
        scratch_shapes=[pltpu.VMEM((tm, d // 2), jnp.uint32), pltpu.SemaphoreType.DMA((1,))],
        input_output_aliases={2: 0},
        compiler_params=_params("arbitrary"),
        name="moe_dispatch")(pos_tiles, y16, zeros)


def _combine_kernel(pos_ref, gates_ref, y_ref, g_ref, b_ref, src_ref, o_ref, o16_ref, buf_ref, sem, *, alpha):
    tm = y_ref.shape[0]

    def copy(r, k):
        return pltpu.make_async_copy(src_ref.at[pl.ds(pos_ref[0, r * TOP_K + k], 1), :],
                                     buf_ref.at[k, pl.ds(r, 1), :], sem.at[0])

    def start(r, carry):
        for k in range(TOP_K):
            copy(r, k).start(priority=k % 2)
        return carry

    def wait(r, carry):
        for k in range(TOP_K):
            copy(0, k).wait()
        return carry

    lax.fori_loop(0, tm, start, 0, unroll=4)
    lax.fori_loop(0, tm, wait, 0, unroll=4)
    gates = gates_ref[...]
    f = gates[:, 0:1] * buf_ref[0]
    for k in range(1, TOP_K):
        f = f + gates[:, k:k + 1] * buf_ref[k]
    y = _layer_norm(alpha * y_ref[...] + f, g_ref[...], b_ref[...])
    o_ref[...] = y
    o16_ref[...] = y.astype(BF16)


def _combine(pos_tiles, gates, y, g, b, out_sorted, alpha):
    t, d = y.shape
    tm = TOKEN_TILE
    row = pl.BlockSpec((tm, d), lambda i: (i, 0))
    vec = pl.BlockSpec((1, d), lambda i: (0, 0))
    return pl.pallas_call(
        functools.partial(_combine_kernel, alpha=alpha),
        out_shape=(jax.ShapeDtypeStruct((t, d), F32), jax.ShapeDtypeStruct((t, d), BF16)),
        grid=(t // tm,),
        in_specs=[pl.BlockSpec((None, 1, tm * TOP_K), lambda i: (i, 0, 0), memory_space=pltpu.SMEM),
                  pl.BlockSpec((tm, TOP_K), lambda i: (i, 0)), row, vec, vec,
                  pl.BlockSpec(memory_space=pl.ANY)],
        out_specs=(row, row),
        scratch_shapes=[pltpu.VMEM((TOP_K, tm, d), F32), pltpu.SemaphoreType.DMA((1,))],
        compiler_params=_params("arbitrary"),
        name="moe_combine_norm")(pos_tiles, gates, y, g, b, out_sorted)

def _moe_up_kernel(te_ref, tv_ref, nw_ref, x_ref, wg_ref, wl_ref, bg_ref, bl_ref, o_ref, wgb_ref, wlb_ref):
    t = pl.program_id(1)

    @pl.when(nw_ref[t] == 1)
    def _():
        wgb_ref[...] = wg_ref[...].astype(BF16)
        wlb_ref[...] = wl_ref[...].astype(BF16)

    @pl.when(tv_ref[t] == 1)
    def _():
        xu = x_ref[...]
        half = xu.shape[1]
        x_lo = pltpu.bitcast(xu << 16, F32).astype(BF16)
        x_hi = pltpu.bitcast(xu & jnp.uint32(0xFFFF0000), F32).astype(BF16)
        hg = _dot(x_lo, wgb_ref[:half, :]) + _dot(x_hi, wgb_ref[half:, :]) + bg_ref[...]
        hl = _dot(x_lo, wlb_ref[:half, :]) + _dot(x_hi, wlb_ref[half:, :]) + bl_ref[...]
        gate = jnp.minimum(hg, SWIGLU_LIMIT)
        lin = jnp.clip(hl, -SWIGLU_LIMIT, SWIGLU_LIMIT)
        act = gate * (1.0 / (1.0 + jnp.exp(-SWIGLU_ALPHA * gate))) * (lin + 1.0)
        o_ref[...] = act.astype(o_ref.dtype)

    @pl.when(tv_ref[t] == 0)
    def _():
        o_ref[...] = jnp.zeros(o_ref.shape, o_ref.dtype)


def _moe_up(tile_e, tile_valid, new_w, xs, w1, b1, layer):
    n_tiles = tile_e.shape[0]
    m = n_tiles * MOE_TILE
    d = w1.shape[2]
    f = w1.shape[3] // 2
    tn = MOE_UP_COLS
    nj = f // tn
    grid_spec = pltpu.PrefetchScalarGridSpec(
        num_scalar_prefetch=3, grid=(nj, n_tiles),
        in_specs=[
            pl.BlockSpec((MOE_TILE, d // 2), lambda j, t, te, tv, nw: (t, 0)),
            pl.BlockSpec((None, None, d, tn), lambda j, t, te, tv, nw: (layer, te[t], 0, j)),
            pl.BlockSpec((None, None, d, tn), lambda j, t, te, tv, nw: (layer, te[t], 0, nj + j)),
            pl.BlockSpec((None, None, 1, tn), lambda j, t, te, tv, nw: (layer, te[t], 0, j)),
            pl.BlockSpec((None, None, 1, tn), lambda j, t, te, tv, nw: (layer, te[t], 0, nj + j)),
        ],
        out_specs=pl.BlockSpec((MOE_TILE, tn), lambda j, t, te, tv, nw: (t, j)),
        scratch_shapes=[pltpu.VMEM((d, tn), BF16), pltpu.VMEM((d, tn), BF16)])
    return pl.pallas_call(
        _moe_up_kernel,
        out_shape=jax.ShapeDtypeStruct((m, f), BF16),
        grid_spec=grid_spec,
        compiler_params=_params("arbitrary", "arbitrary"),
        name="moe_up")(tile_e, tile_valid, new_w, xs, w1, w1, b1, b1)


def _moe_down_kernel(te_ref, tv_ref, nw_ref, a_ref, w_ref, b_ref, o_ref, wb_ref):
    t = pl.program_id(1)

    @pl.when(nw_ref[t] == 1)
    def _():
        wb_ref[...] = w_ref[...].astype(BF16)

    @pl.when(tv_ref[t] == 1)
    def _():
        o_ref[...] = _dot(a_ref[...], wb_ref[...]) + b_ref[...]

    @pl.when(tv_ref[t] == 0)
    def _():
        o_ref[...] = jnp.zeros(o_ref.shape, o_ref.dtype)


def _moe_down(tile_e, tile_valid, new_w, act, w2, b2, layer):
    m, f = act.shape
    d = w2.shape[3]
    tn = MOE_DOWN_COLS
    grid_spec = pltpu.PrefetchScalarGridSpec(
        num_scalar_prefetch=3, grid=(d // tn, m // MOE_TILE),
        in_specs=[
            pl.BlockSpec((MOE_TILE, f), lambda j, t, te, tv, nw: (t, 0)),
            pl.BlockSpec((None, None, f, tn), lambda j, t, te, tv, nw: (layer, te[t], 0, j)),
            pl.BlockSpec((None, None, 1, tn), lambda j, t, te, tv, nw: (layer, te[t], 0, j)),
        ],
        out_specs=pl.BlockSpec((MOE_TILE, tn), lambda j, t, te, tv, nw: (t, j)),
        scratch_shapes=[pltpu.VMEM((f, tn), BF16)])
    return pl.pallas_call(
        _moe_down_kernel,
        out_shape=jax.ShapeDtypeStruct((m, d), F32),
        grid_spec=grid_spec,
        compiler_params=_params("arbitrary", "arbitrary"),
        name="moe_down")(tile_e, tile_valid, new_w, act, w2, b2)


def _route(logits):
    t = logits.shape[0]
    top_val, top_idx = lax.top_k(logits, TOP_K)
    gates = jax.nn.softmax(top_val, axis=-1)
    top_idx = top_idx.astype(jnp.int32)
    experts = jnp.arange(N_EXPERTS, dtype=jnp.int32)
    onehot = jnp.sum((top_idx[:, :, None] == experts[None, None, :]).astype(jnp.int32), axis=1)
    csum = jnp.cumsum(onehot, axis=0)
    counts = csum[-1]
    rank = jnp.take_along_axis(csum - onehot, top_idx, axis=1)
    padded = (counts + MOE_TILE - 1) // MOE_TILE * MOE_TILE
    pend = jnp.cumsum(padded)
    pstart = pend - padded
    pos = pstart[top_idx] + rank
    n_tiles = -(-(t * TOP_K + N_EXPERTS * (MOE_TILE - 1)) // MOE_TILE)
    tile_start = jnp.arange(n_tiles, dtype=jnp.int32) * MOE_TILE
    valid = tile_start < pend[-1]
    raw = jnp.minimum(jnp.sum((tile_start[:, None] >= pend[None, :]).astype(jnp.int32), axis=1), N_EXPERTS - 1)
    tile_e = jnp.where(valid, raw, jnp.max(jnp.where(valid, raw, 0)))
    new_w = jnp.concatenate([jnp.ones((1,), jnp.int32), (tile_e[1:] != tile_e[:-1]).astype(jnp.int32)])
    return gates, pos, tile_e, valid.astype(jnp.int32), new_w


def _alibi_slopes(n):
    return 2.0 ** (-8.0 * jnp.arange(1, n + 1, dtype=F32) / n)


def kernel(x_prompt, x_sample, cache_fox_k, cache_fox_v, cache_fox_logf, cache_win_k, cache_win_v,
           cache_mem_k, cache_mem_v, page_table, mem_prompt, w_in_a, b_fgate, w_in_b, w_mem_kv, w_out,
           ln1_g, ln1_b, router_w, router_b, w1, b1, w2, b2, ln2_g, ln2_b):
    B, S, D = x_prompt.shape
    DB, n, _ = x_sample.shape
    depth = w_out.shape[0]
    n_mem = mem_prompt.shape[1]
    n_prompt = B * S
    n_tok = n_prompt + DB * n
    t_pad = -(-n_tok // ROW_TILE) * ROW_TILE
    s_blk = n_prompt // ROW_TILE
    assert n_prompt % ROW_TILE == 0 and DB * n <= ROW_TILE and n <= SUBLANES
    alpha = float((2 * depth) ** 0.25)
    slopes = _alibi_slopes(N_SELF_HEADS)
    n_pool, page = cache_fox_k.shape[1], cache_fox_k.shape[2]

    tok = jnp.concatenate([x_prompt.reshape(n_prompt, D), x_sample.reshape(DB * n, D),
                           jnp.zeros((t_pad - n_tok, D), F32)], axis=0)
    tok16 = tok.astype(BF16)

    mem_k_p, mem_v_p = _mem_kv(mem_prompt.reshape(B * n_mem, D), w_mem_kv)
    mem_k_p = mem_k_p.reshape(depth, B, n_mem, MEM_W)
    mem_v_p = mem_v_p.reshape(depth, B, n_mem, MEM_W)
    cmk = cache_mem_k.reshape(depth, DB, n_mem, MEM_W)
    cmv = cache_mem_v.reshape(depth, DB, n_mem, MEM_W)

    kc = cache_fox_k.transpose(0, 1, 3, 2, 4)
    vc = cache_fox_v.transpose(0, 1, 3, 2, 4)
    lfc = cache_fox_logf.transpose(0, 1, 3, 2)
    win_rows = cache_win_k.shape[2]
    wk = cache_win_k.reshape(cache_win_k.shape[0], DB, win_rows, GROUP_W)
    wv = cache_win_v.reshape(cache_win_v.shape[0], DB, win_rows, GROUP_W)

    w_out16 = w_out.astype(BF16).reshape(depth, 4, GROUP_W, D)
    b1r = b1.reshape(b1.shape[0], b1.shape[1], 1, b1.shape[2])
    b2r = b2.reshape(b2.shape[0], b2.shape[1], 1, b2.shape[2])

    def sample_rows(arr, c0, width):
        r = arr[n_prompt:n_tok, c0:c0 + width].reshape(DB, n, width)
        return jnp.pad(r, ((0, 0), (0, SUBLANES - n), (0, 0)))

    def put_sample(table, rows):
        width = table.shape[1]
        blk = jnp.zeros((t_pad - n_prompt, width), table.dtype).at[:DB * n].set(
            rows.reshape(DB * n, width).astype(table.dtype))
        return jnp.concatenate([table, blk], axis=0)

    fk_p, fv_p, flf_p, fk_s, fv_s, flf_s = [], [], [], [], [], []
    wk_p, wv_p, wk_s, wv_s = [], [], [], []
    for i in range(depth):
        j = i // N_MIXERS
        if i % N_MIXERS == 0:
            qkv = _proj(tok16, w_in_a, j, 3 * SELF_W)
            w_f = w_in_a[j][:, 3 * SELF_W:3 * SELF_W + N_SELF_HEADS]
            w_qm = w_in_a[j][:, 3 * SELF_W + N_SELF_HEADS:]
            qm = _proj(tok16, w_qm[None], 0, MEM_W)
            qm_col0 = 0
            hp = 2 * SUBLANES
            wf_t = jnp.pad(w_f.T, ((0, hp - N_SELF_HEADS), (0, 0))).astype(BF16)
            b_col = jnp.pad(b_fgate[j], (0, hp - N_SELF_HEADS)).reshape(hp, 1)
            lf_p, c_p = _gate(tok16, wf_t, b_col, S, 0, B)
            lf_s, _ = _gate(tok16, wf_t, b_col, ROW_TILE, s_blk, 1)
            ck = c_p[:, :N_SELF_HEADS].reshape(B, N_SELF_HEADS, 1, S)
            o_self = _fox_prompt(qkv, ck, n_prompt, B, S)
            q_s = sample_rows(qkv, 0, SELF_W)
            k_s = sample_rows(qkv, SELF_W, SELF_W)
            v_s = sample_rows(qkv, 2 * SELF_W, SELF_W)
            lf_s = lf_s[0, :N_SELF_HEADS, :DB * n].reshape(N_SELF_HEADS, DB, n).transpose(1, 0, 2)
            def head_major(x, rows):
                xh = x.reshape(DB, SUBLANES, N_SELF_HEADS, HEAD_DIM).transpose(0, 2, 1, 3)
                return jnp.pad(xh, ((0, 0), (0, 0), (0, rows - SUBLANES), (0, 0)))

            q16 = head_major(q_s, HEAD_ROWS).reshape(DB, N_SELF_HEADS * HEAD_ROWS, HEAD_DIM).astype(BF16)
            lfnew = jnp.pad(lf_s, ((0, 0), (0, 0), (0, page - n)))
            o_s = _fox_decode(page_table, q16, kc, vc, lfc, j, head_major(k_s, page), head_major(v_s, page),
                              lfnew, n)
            o_s = o_s.reshape(DB, N_SELF_HEADS, HEAD_ROWS, HEAD_DIM)[:, :, :n].transpose(0, 2, 1, 3)
            o_self = put_sample(o_self, o_s.reshape(DB, n, SELF_W))
            xs_parts, x_cols = [o_self, o_self, o_self], [0, 1, 2]
            fk_p.append(qkv[:n_prompt, SELF_W:2 * SELF_W].reshape(B, S, N_SELF_HEADS, HEAD_DIM))
            fv_p.append(qkv[:n_prompt, 2 * SELF_W:3 * SELF_W].reshape(B, S, N_SELF_HEADS, HEAD_DIM))
            flf_p.append(lf_p[:, :N_SELF_HEADS].transpose(0, 2, 1))
            fk_s.append(k_s[:, :n].reshape(DB, n, N_SELF_HEADS, HEAD_DIM))
            fv_s.append(v_s[:, :n].reshape(DB, n, N_SELF_HEADS, HEAD_DIM))
            flf_s.append(lf_s.transpose(0, 2, 1))
        else:
            qkv = _proj(tok16, w_in_b, j, 3 * SELF_W + MEM_W)
            qm, qm_col0 = qkv, 3 * N_SELF_HEADS
            o_groups = _dil_prompt(qkv, slopes, n_prompt, B, S)
            q_s = sample_rows(qkv, 0, SELF_W)
            k_s = sample_rows(qkv, SELF_W, SELF_W)
            v_s = sample_rows(qkv, 2 * SELF_W, SELF_W)
            o_s, nk, nv = _dil_sample(slopes, q_s, k_s, v_s, wk, wv, j, n, page_table.shape[1] * page)
            xs_parts = [put_sample(o_groups[g], o_s[g][:, :n]) for g in range(len(DIL_GROUPS))]
            x_cols = [0, 0, 0]
            kp = qkv[:n_prompt, SELF_W:2 * SELF_W].reshape(B, S, N_SELF_HEADS, HEAD_DIM)
            vp = qkv[:n_prompt, 2 * SELF_W:3 * SELF_W].reshape(B, S, N_SELF_HEADS, HEAD_DIM)
            bk, bv = [], []
            for g, (w, d) in enumerate(DIL_GROUPS):
                hs = slice(g * HEADS_PER_GROUP, (g + 1) * HEADS_PER_GROUP)
                pad = ((0, 0), (w, 0), (0, 0), (0, 0))
                bk.append(jnp.pad(kp[:, :, hs], pad)[:, -w:])
                bv.append(jnp.pad(vp[:, :, hs], pad)[:, -w:])
            wk_p.append(jnp.concatenate(bk, axis=1))
            wv_p.append(jnp.concatenate(bv, axis=1))
            wk_s.append(nk.reshape(DB, win_rows, HEADS_PER_GROUP, HEAD_DIM))
            wv_s.append(nv.reshape(DB, win_rows, HEADS_PER_GROUP, HEAD_DIM))
        o_mem = _mem_attn(qm, qm_col0, mem_k_p, mem_v_p, i, B, S, ATTN_TILE, n_prompt, BF16)
        qm_s = sample_rows(qm, qm_col0 * HEAD_DIM, MEM_W).reshape(DB * SUBLANES, MEM_W)
        om_s = _mem_attn(qm_s, 0, cmk, cmv, i, DB, SUBLANES, SUBLANES, DB * SUBLANES, F32)
        o_mem = put_sample(o_mem, om_s.reshape(DB, SUBLANES, MEM_W)[:, :n])
        rw = jnp.pad(router_w[i], ((0, 0), (0, LANES - N_EXPERTS)))
        rw_hi = rw.astype(BF16)
        rw_lo = (rw - rw_hi.astype(F32)).astype(BF16)
        rb = jnp.pad(router_b[i], (0, LANES - N_EXPERTS)).reshape(1, LANES)
        y1, y16, logits = _oproj(xs_parts + [o_mem], x_cols + [0], w_out16, i, tok,
                                 ln1_g[i].reshape(1, D), ln1_b[i].reshape(1, D), rw_hi, rw_lo, rb, alpha)
        gates, pos, tile_e, tile_valid, new_w = _route(logits[:n_tok, :N_EXPERTS])
        gates = jnp.pad(gates, ((0, t_pad - n_tok), (0, 0)))
        pos_tiles = jnp.pad(pos, ((0, t_pad - n_tok), (0, 0))).reshape(
            t_pad // TOKEN_TILE, 1, TOKEN_TILE * TOP_K)
        xs = _dispatch(pos_tiles, y16, tile_e.shape[0] * MOE_TILE, n_tok)
        act = _moe_up(tile_e, tile_valid, new_w, xs, w1, b1r, i)
        out_sorted = _moe_down(tile_e, tile_valid, new_w, act, w2, b2r, i)
        tok, tok16 = _combine(pos_tiles, gates, y1, ln2_g[i].reshape(1, D), ln2_b[i].reshape(1, D),
                              out_sorted, alpha)

    yp = tok[:n_prompt].reshape(B, S, D)
    ys = tok[n_prompt:n_tok].reshape(DB, n, D)
    mk_out = mem_k_p.reshape(depth, B, n_mem, N_MEM_HEADS, HEAD_DIM)
    mv_out = mem_v_p.reshape(depth, B, n_mem, N_MEM_HEADS, HEAD_DIM)
    return (yp, ys, jnp.stack(fk_p), jnp.stack(fv_p), jnp.stack(flf_p), jnp.stack(wk_p), jnp.stack(wv_p),
            mk_out, mv_out, jnp.stack(fk_s), jnp.stack(fv_s), jnp.stack(flf_s), jnp.stack(wk_s), jnp.stack(wv_s))
```
